```python
import math
import jax, jax.numpy as jnp
from jax import lax
import numpy as np

D_MODEL = 1024
BATCH = 16
SEQ = 2048
DEPTH = 2

N_MIXERS = 2
N_A_LAYERS = (DEPTH + 1) // 2
N_B_LAYERS = DEPTH // 2

S5_WIDTH = D_MODEL // 2
S5_GROUP = 16
S5_GROUPS = S5_WIDTH // S5_GROUP
S5_STATE = 64
DT_MIN = 0.001
DT_MAX = 0.1

SB_HEAD_DIM = 64
SB_HEADS = D_MODEL // SB_HEAD_DIM
Q_BLOCK = 128

D_FF = 7 * D_MODEL // 2
N_EXPERTS = 8
TOP_K = 2
NORM_EPS = 1e-6

kernel_name = "hybrid_s5_stickbreaking_moe_sandwich_adaln"


def rmsnorm(x, g):
    xf = x.astype(jnp.float32)
    xf = xf * lax.rsqrt(jnp.mean(xf * xf, axis=-1, keepdims=True) + NORM_EPS)
    return xf.astype(x.dtype) * g


def swiglu(h, w_gate_up, w_down):
    gate, up = jnp.split(h @ w_gate_up, 2, axis=-1)
    return (jax.nn.silu(gate) * up) @ w_down


def _complex_linear_combine(left, right):
    a1r, a1i, b1r, b1i = left
    a2r, a2i, b2r, b2i = right
    return (a2r * a1r - a2i * a1i,
            a2r * a1i + a2i * a1r,
            a2r * b1r - a2i * b1i + b2r,
            a2r * b1i + a2i * b1r + b2i)


def s5_mixer(h, w_in, lam_re, lam_im, b_re, b_im, c_re, c_im, d_skip, log_step, w_out):
    bsz, seq, _ = h.shape
    f32 = jnp.float32
    lam_re, lam_im = lam_re.astype(f32), lam_im.astype(f32)
    b_re, b_im = b_re.astype(f32), b_im.astype(f32)
    c_re, c_im = c_re.astype(f32), c_im.astype(f32)
    u = (h @ w_in).astype(f32).reshape(bsz, seq, S5_GROUPS, S5_GROUP)
    step = jnp.exp(log_step.astype(f32))[:, None]
    mag = jnp.exp(lam_re * step)
    lb_re = mag * jnp.cos(lam_im * step)
    lb_im = mag * jnp.sin(lam_im * step)
    den = lam_re * lam_re + lam_im * lam_im
    coef_re = ((lb_re - 1.0) * lam_re + lb_im * lam_im) / den
    coef_im = (lb_im * lam_re - (lb_re - 1.0) * lam_im) / den
    bb_re = coef_re[..., None] * b_re - coef_im[..., None] * b_im
    bb_im = coef_re[..., None] * b_im + coef_im[..., None] * b_re
    bu_re = jnp.einsum("blgh,gph->blgp", u, bb_re)
    bu_im = jnp.einsum("blgh,gph->blgp", u, bb_im)
    a_re = jnp.broadcast_to(lb_re, (1, seq) + lb_re.shape)
    a_im = jnp.broadcast_to(lb_im, (1, seq) + lb_im.shape)
    _, _, s_re, s_im = lax.associative_scan(
        _complex_linear_combine, (a_re, a_im, bu_re, bu_im), axis=1)
    y = (jnp.einsum("blgp,ghp->blgh", s_re, c_re)
         - jnp.einsum("blgp,ghp->blgh", s_im, c_im)
         + d_skip.astype(f32) * u)
    y = jax.nn.gelu(y.reshape(bsz, seq, S5_WIDTH)).astype(h.dtype)
    val, gate = jnp.split(y @ w_out, 2, axis=-1)
    return val * jax.nn.sigmoid(gate)


def stick_breaking_attention(h, w_qkv, w_out):
    bsz, seq, _ = h.shape
    qkv = (h @ w_qkv).reshape(bsz, seq, 3, SB_HEADS, SB_HEAD_DIM)
    q = jnp.transpose(qkv[:, :, 0], (0, 2, 1, 3))
    k = jnp.transpose(qkv[:, :, 1], (0, 2, 1, 3))
    v = jnp.transpose(qkv[:, :, 2], (0, 2, 1, 3))
    scale = 1.0 / math.sqrt(SB_HEAD_DIM)
    out_blocks = []
    for blk in range(seq // Q_BLOCK):
        t0 = blk * Q_BLOCK
        n_keys = t0 + Q_BLOCK
        z = jnp.einsum("bhqd,bhkd->bhqk", q[:, :, t0:n_keys],
                       k[:, :, :n_keys]).astype(jnp.float32) * scale
        t_pos = t0 + jnp.arange(Q_BLOCK)[:, None]
        s_pos = jnp.arange(n_keys)[None, :]
        strict = s_pos < t_pos
        log_keep = jnp.where(strict, jax.nn.log_sigmoid(-z), 0.0)
        log_keep_after = lax.cumsum(log_keep, axis=3, reverse=True) - log_keep
        weight = jnp.where(strict, jnp.exp(jax.nn.log_sigmoid(z) + log_keep_after), 0.0)
        out_blocks.append(jnp.einsum("bhqk,bhkd->bhqd", weight.astype(v.dtype),
                                     v[:, :, :n_keys]))
    o = jnp.concatenate(out_blocks, axis=2)
    o = jnp.transpose(o, (0, 2, 1, 3)).reshape(bsz, seq, D_MODEL)
    return o @ w_out


def moe_swiglu(h, w_router, b_router, w_gate_up, w_down):
    bsz, seq, d = h.shape
    tok = h.reshape(bsz * seq, d)
    logits = (tok @ w_router).astype(jnp.float32) + b_router.astype(jnp.float32)
    probs = jax.nn.softmax(logits, axis=-1)
    top_p, top_i = lax.top_k(probs, TOP_K)
    top_p = top_p / jnp.sum(top_p, axis=-1, keepdims=True)
    gates = jnp.sum(jax.nn.one_hot(top_i, N_EXPERTS, dtype=jnp.float32)
                    * top_p[..., None], axis=1)
    out = jnp.zeros_like(tok)
    for e in range(N_EXPERTS):
        out = out + gates[:, e:e + 1].astype(tok.dtype) * swiglu(tok, w_gate_up[e], w_down[e])
    return out.reshape(bsz, seq, d)


def setup_inputs(seed: int = 0) -> dict:
    key = jax.random.key(seed)
    ks = jax.random.split(key, 24)
    f32 = jnp.float32

    def nrm(k, shape, std):
        return std * jax.random.normal(k, shape, f32)

    d, w, g, p, hh = D_MODEL, S5_WIDTH, S5_GROUPS, S5_STATE, S5_GROUP
    return {
        "x": nrm(ks[0], (BATCH, SEQ, d), 1.0),
        "c": nrm(ks[1], (BATCH, d), 1.0),
        "ada_w": nrm(ks[2], (DEPTH, d, 6 * d), d ** -0.5),
        "ada_b": nrm(ks[3], (DEPTH, 6 * d), 0.02),
        "norm_g": 1.0 + nrm(ks[4], (DEPTH, 4, d), 0.02),
        "s5_w_in": nrm(ks[5], (N_A_LAYERS, d, w), d ** -0.5),
        "s5_lambda_re": -0.5 + nrm(ks[6], (N_A_LAYERS, g, p), 0.01),
        "s5_lambda_im": math.pi * jnp.arange(p, dtype=f32) + nrm(ks[7], (N_A_LAYERS, g, p), 0.01),
        "s5_b_re": nrm(ks[8], (N_A_LAYERS, g, p, hh), (2.0 * hh) ** -0.5),
        "s5_b_im": nrm(ks[9], (N_A_LAYERS, g, p, hh), (2.0 * hh) ** -0.5),
        "s5_c_re": nrm(ks[10], (N_A_LAYERS, g, hh, p), 0.5 ** 0.5),
        "s5_c_im": nrm(ks[11], (N_A_LAYERS, g, hh, p), 0.5 ** 0.5),
        "s5_d": nrm(ks[12], (N_A_LAYERS, g, hh), 1.0),
        "s5_log_step": jax.random.uniform(ks[13], (N_A_LAYERS, g), f32,
                                          math.log(DT_MIN), math.log(DT_MAX)),
        "s5_w_out": nrm(ks[14], (N_A_LAYERS, w, 2 * d), w ** -0.5),
        "sb_w_qkv": nrm(ks[15], (N_B_LAYERS, d, 3 * d), d ** -0.5),
        "sb_w_out": nrm(ks[16], (N_B_LAYERS, d, d), d ** -0.5),
        "ffn_w_gate_up": nrm(ks[17], (N_A_LAYERS, d, 2 * D_FF), d ** -0.5),
        "ffn_w_down": nrm(ks[18], (N_A_LAYERS, D_FF, d), D_FF ** -0.5),
        "moe_w_router": nrm(ks[19], (N_B_LAYERS, d, N_EXPERTS), d ** -0.5),
        "moe_b_router": nrm(ks[20], (N_B_LAYERS, N_EXPERTS), 0.01),
        "moe_w_gate_up": nrm(ks[21], (N_B_LAYERS, N_EXPERTS, d, 2 * D_FF), d ** -0.5),
        "moe_w_down": nrm(ks[22], (N_B_LAYERS, N_EXPERTS, D_FF, d), D_FF ** -0.5),
    }


def reference(x, c, ada_w, ada_b, norm_g, s5_w_in, s5_lambda_re, s5_lambda_im,
              s5_b_re, s5_b_im, s5_c_re, s5_c_im, s5_d, s5_log_step, s5_w_out,
              sb_w_qkv, sb_w_out, ffn_w_gate_up, ffn_w_down,
              moe_w_router, moe_b_router, moe_w_gate_up, moe_w_down):
    c_act = jax.nn.silu(c)
    for i in range(DEPTH):
        j = i // N_MIXERS
        mod = (c_act @ ada_w[i] + ada_b[i])[:, None, :]
        shift_m, scale_m, gate_m, shift_f, scale_f, gate_f = jnp.split(mod, 6, axis=-1)
        h = rmsnorm(x, norm_g[i, 0]) * (1.0 + scale_m) + shift_m
        if i % N_MIXERS == 0:
            y = s5_mixer(h, s5_w_in[j], s5_lambda_re[j], s5_lambda_im[j], s5_b_re[j],
                         s5_b_im[j], s5_c_re[j], s5_c_im[j], s5_d[j], s5_log_step[j],
                         s5_w_out[j])
        else:
            y = stick_breaking_attention(h, sb_w_qkv[j], sb_w_out[j])
        x = x + gate_m * rmsnorm(y, norm_g[i, 1])
        h = rmsnorm(x, norm_g[i, 2]) * (1.0 + scale_f) + shift_f
        if i % 2 == 0:
            y = swiglu(h, ffn_w_gate_up[j], ffn_w_down[j])
        else:
            y = moe_swiglu(h, moe_w_router[j], moe_b_router[j], moe_w_gate_up[j], moe_w_down[j])
        x = x + gate_f * rmsnorm(y, norm_g[i, 3])
    return x
```

```python
import functools
import math

import jax
import jax.numpy as jnp
from jax import lax
from jax.experimental import pallas as pl
from jax.experimental.pallas import tpu as pltpu

F32 = jnp.float32
BF16 = jnp.bfloat16

NORM_EPS = 1e-6
LANES = 128
MIB = 1024 * 1024


def _cparams(semantics, vmem_mib):
    return pltpu.CompilerParams(dimension_semantics=semantics,
                                vmem_limit_bytes=vmem_mib * MIB)


def _rms(x):
    return x * lax.rsqrt(jnp.mean(x * x, axis=-1, keepdims=True) + NORM_EPS)


def _norm_mod(x, g, scale, shift):
    return (_rms(x) * g) * (1.0 + scale) + shift


def _silu(x):
    return x * jax.nn.sigmoid(x)


def _bdot(a, b):
    return jnp.dot(a, b, preferred_element_type=F32)


def _adaln_kernel(c_ref, w_ref, b_ref, o_ref):
    ca = _silu(c_ref[...]).astype(BF16)
    o_ref[...] = _bdot(ca, w_ref[...].astype(BF16)) + b_ref[...]


def _adaln(c, ada_w, ada_b):
    depth, d, six_d = ada_w.shape
    bsz = c.shape[0]
    n6 = six_d // d
    return pl.pallas_call(
        _adaln_kernel,
        grid=(depth, n6),
        in_specs=[pl.BlockSpec((bsz, d), lambda l, k: (0, 0)),
                  pl.BlockSpec((None, d, d), lambda l, k: (l, 0, k)),
                  pl.BlockSpec((None, None, 1, d), lambda l, k: (l, k, 0, 0))],
        out_specs=pl.BlockSpec((None, None, bsz, d), lambda l, k: (l, k, 0, 0)),
        out_shape=jax.ShapeDtypeStruct((depth, n6, bsz, d), F32),
        compiler_params=_cparams(("arbitrary", "arbitrary"), 24),
        name="adaln",
    )(c, ada_w, ada_b.reshape(depth, n6, 1, d))


def _nm_mm_kernel(x_ref, g_ref, sc_ref, sh_ref, w_ref, o_ref, *, n_chunk):
    h = _norm_mod(x_ref[...], g_ref[...], sc_ref[...], sh_ref[...]).astype(BF16)
    for n0 in range(0, o_ref.shape[1], n_chunk):
        o_ref[:, n0:n0 + n_chunk] = _bdot(h, w_ref[:, n0:n0 + n_chunk]).astype(o_ref.dtype)


def _nm_mm(x2, ng, mod5, layer, w, *, seq, tm, out_dtype):
    t, d = x2.shape
    n = w.shape[1]
    per_b = seq // tm
    mod_spec = lambda k: pl.BlockSpec((None, None, None, 1, d),
                                      lambda i: (layer, k, i // per_b, 0, 0))
    return pl.pallas_call(
        functools.partial(_nm_mm_kernel, n_chunk=min(n, 1024)),
        grid=(t // tm,),
        in_specs=[pl.BlockSpec((tm, d), lambda i: (i, 0)),
                  pl.BlockSpec((None, 1, d), lambda i: (layer * 4, 0, 0)),
                  mod_spec(1), mod_spec(0),
                  pl.BlockSpec((d, n), lambda i: (0, 0))],
        out_specs=pl.BlockSpec((tm, n), lambda i: (i, 0)),
        out_shape=jax.ShapeDtypeStruct((t, n), out_dtype),
        compiler_params=_cparams(("arbitrary",), 48),
        name="norm_mod_matmul",
    )(x2, ng, mod5, mod5, w)


def _s5_in_kernel(x_ref, g_ref, sc_ref, sh_ref, w_ref, o_ref):
    nb = x_ref.shape[0]
    wd = w_ref.shape[1]
    for b in range(nb):
        h = _norm_mod(x_ref[b], g_ref[...], sc_ref[b:b + 1, :], sh_ref[b:b + 1, :])
        o_ref[:, b * wd:(b + 1) * wd] = _bdot(h.astype(BF16), w_ref[...])


def _s5_in(x, ng, mod, w_in, *, tt):
    bsz, seq, d = x.shape
    wd = w_in.shape[1]
    mod_spec = lambda k: pl.BlockSpec((None, None, bsz, d), lambda i: (0, k, 0, 0))
    return pl.pallas_call(
        _s5_in_kernel,
        grid=(seq // tt,),
        in_specs=[pl.BlockSpec((bsz, tt, d), lambda i: (0, i, 0)),
                  pl.BlockSpec((None, 1, d), lambda i: (0, 0, 0)),
                  mod_spec(1), mod_spec(0),
                  pl.BlockSpec((d, wd), lambda i: (0, 0))],
        out_specs=pl.BlockSpec((tt, bsz * wd), lambda i: (i, 0)),
        out_shape=jax.ShapeDtypeStruct((seq, bsz * wd), F32),
        compiler_params=_cparams(("arbitrary",), 40),
        name="s5_in",
    )(x, ng, mod, mod, w_in)


def _ssm_kernel(u_ref, bbd_ref, cbd_ref, lam_ref, d_ref, y_ref, bs_ref, st_ref, *, nb, tc):
    n_slab = bbd_ref.shape[0]
    half = bbd_ref.shape[2] // 2

    @pl.when(pl.program_id(0) == 0)
    def _():
        st_ref[...] = jnp.zeros_like(st_ref)

    for j in range(n_slab):
        lanes = slice(j * LANES, (j + 1) * LANES)
        uj = u_ref[:, lanes]
        bs_ref[...] = _bdot(uj.astype(BF16), bbd_ref[j])
        lr = jnp.broadcast_to(lam_ref[j, 0:1, :], (nb, half))
        li = jnp.broadcast_to(lam_ref[j, 1:2, :], (nb, half))

        def step(t, carry):
            sr, si = carry
            r0 = pl.multiple_of(t * nb, nb)
            rows = pl.ds(r0, nb)
            nr = lr * sr - li * si + bs_ref[rows, :half]
            ni = lr * si + li * sr + bs_ref[rows, half:]
            bs_ref[rows, :half] = nr
            bs_ref[rows, half:] = ni
            return nr, ni

        sr, si = lax.fori_loop(0, tc, step, (st_ref[j, :, :half], st_ref[j, :, half:]),
                               unroll=4)
        st_ref[j, :, :half] = sr
        st_ref[j, :, half:] = si
        yj = _bdot(bs_ref[...].astype(BF16), cbd_ref[j]) + d_ref[:, lanes] * uj
        y_ref[:, lanes] = jax.nn.gelu(yj).astype(y_ref.dtype)


def _ssm(u2, bbd, cbd, lam, d_skip, *, nb, tc):
    rows, wd = u2.shape
    n_slab, _, two_p = bbd.shape
    tr = tc * nb
    return pl.pallas_call(
        functools.partial(_ssm_kernel, nb=nb, tc=tc),
        grid=(rows // tr,),
        in_specs=[pl.BlockSpec((tr, wd), lambda i: (i, 0)),
                  pl.BlockSpec(bbd.shape, lambda i: (0, 0, 0)),
                  pl.BlockSpec(cbd.shape, lambda i: (0, 0, 0)),
                  pl.BlockSpec(lam.shape, lambda i: (0, 0, 0)),
                  pl.BlockSpec((1, wd), lambda i: (0, 0))],
        out_specs=pl.BlockSpec((tr, wd), lambda i: (i, 0)),
        out_shape=jax.ShapeDtypeStruct((rows, wd), BF16),
        scratch_shapes=[pltpu.VMEM((tr, two_p), F32),
                        pltpu.VMEM((n_slab, nb, two_p), F32)],
        compiler_params=_cparams(("arbitrary",), 48),
        name="s5_scan",
    )(u2, bbd, cbd, lam, d_skip)


def _s5_discretise(lam_re, lam_im, b_re, b_im, c_re, c_im, d_skip, log_step):
    g, p, hh = b_re.shape
    gs = LANES // hh
    n_slab = g // gs
    step = jnp.exp(log_step.astype(F32))[:, None]
    lam_re, lam_im = lam_re.astype(F32), lam_im.astype(F32)
    mag = jnp.exp(lam_re * step)
    lb_re = mag * jnp.cos(lam_im * step)
    lb_im = mag * jnp.sin(lam_im * step)
    den = lam_re * lam_re + lam_im * lam_im
    coef_re = ((lb_re - 1.0) * lam_re + lb_im * lam_im) / den
    coef_im = (lb_im * lam_re - (lb_re - 1.0) * lam_im) / den
    bb_re = coef_re[..., None] * b_re - coef_im[..., None] * b_im
    bb_im = coef_re[..., None] * b_im + coef_im[..., None] * b_re
    eye = jnp.eye(gs, dtype=F32)

    def blockdiag_in(bb):
        t = bb.reshape(n_slab, gs, p, hh).transpose(0, 1, 3, 2)
        return jnp.einsum("jghp,gk->jghkp", t, eye).reshape(n_slab, gs * hh, gs * p)

    def blockdiag_out(cc):
        t = cc.reshape(n_slab, gs, hh, p).transpose(0, 1, 3, 2)
        return jnp.einsum("jgph,gk->jgpkh", t, eye).reshape(n_slab, gs * p, gs * hh)

    bbd = jnp.concatenate([blockdiag_in(bb_re), blockdiag_in(bb_im)], axis=-1).astype(BF16)
    cbd = jnp.concatenate([blockdiag_out(c_re.astype(F32)),
                           -blockdiag_out(c_im.astype(F32))], axis=1).astype(BF16)
    lam = jnp.stack([lb_re.reshape(n_slab, gs * p), lb_im.reshape(n_slab, gs * p)], axis=1)
    return bbd, cbd, lam, d_skip.astype(F32).reshape(1, g * hh)


def _s5_out_kernel(y_ref, w_ref, x_ref, g_ref, gate_ref, o_ref):
    nb = x_ref.shape[0]
    wd = w_ref.shape[0]
    d = x_ref.shape[2]
    for b in range(nb):
        z = _bdot(y_ref[:, b * wd:(b + 1) * wd], w_ref[...])
        v = z[:, :d] * jax.nn.sigmoid(z[:, d:])
        o_ref[b] = x_ref[b] + gate_ref[b:b + 1, :] * (_rms(v) * g_ref[...])


def _s5_out(y, w_out, x, ng, mod, *, tt):
    bsz, seq, d = x.shape
    wd = w_out.shape[0]
    return pl.pallas_call(
        _s5_out_kernel,
        grid=(seq // tt,),
        in_specs=[pl.BlockSpec((tt, bsz * wd), lambda i: (i, 0)),
                  pl.BlockSpec(w_out.shape, lambda i: (0, 0)),
                  pl.BlockSpec((bsz, tt, d), lambda i: (0, i, 0)),
                  pl.BlockSpec((None, 1, d), lambda i: (1, 0, 0)),
                  pl.BlockSpec((None, None, bsz, d), lambda i: (0, 2, 0, 0))],
        out_specs=pl.BlockSpec((bsz, tt, d), lambda i: (0, i, 0)),
        out_shape=jax.ShapeDtypeStruct(x.shape, F32),
        compiler_params=_cparams(("arbitrary",), 48),
        name="s5_out",
    )(y, w_out, x, ng, mod)


def _mm_post_kernel(a_ref, w_ref, x_ref, g_ref, gate_ref, o_ref):
    y = _bdot(a_ref[...], w_ref[...])
    o_ref[...] = x_ref[...] + gate_ref[...] * (_rms(y) * g_ref[...])


def _mm_post(a, w, x2, ng, mod5, layer, *, seq, tm):
    t, d = x2.shape
    k = a.shape[1]
    per_b = seq // tm
    return pl.pallas_call(
        _mm_post_kernel,
        grid=(t // tm,),
        in_specs=[pl.BlockSpec((tm, k), lambda i: (i, 0)),
                  pl.BlockSpec(w.shape, lambda i: (0, 0)),
                  pl.BlockSpec((tm, d), lambda i: (i, 0)),
                  pl.BlockSpec((None, 1, d), lambda i: (layer * 4 + 1, 0, 0)),
                  pl.BlockSpec((None, None, None, 1, d),
                               lambda i: (layer, 2, i // per_b, 0, 0))],
        out_specs=pl.BlockSpec((tm, d), lambda i: (i, 0)),
        out_shape=jax.ShapeDtypeStruct((t, d), F32),
        compiler_params=_cparams(("arbitrary",), 40),
        name="matmul_post",
    )(a, w, x2, ng, mod5)


def _ffn_kernel(x_ref, g2_ref, sc_ref, sh_ref, wg_ref, wu_ref, wd_ref, g3_ref, gate_ref,
                o_ref, h_scr, acc_scr):
    f = pl.program_id(1)

    @pl.when(f == 0)
    def _():
        h_scr[...] = _norm_mod(x_ref[...], g2_ref[...], sc_ref[...], sh_ref[...]).astype(BF16)
        acc_scr[...] = jnp.zeros_like(acc_scr)

    h = h_scr[...]
    gt = _bdot(h, wg_ref[...].astype(BF16))
    up = _bdot(h, wu_ref[...].astype(BF16))
    act = (_silu(gt) * up).astype(BF16)
    acc_scr[...] += _bdot(act, wd_ref[...].astype(BF16))

    @pl.when(f == pl.num_programs(1) - 1)
    def _():
        o_ref[...] = x_ref[...] + gate_ref[...] * (_rms(acc_scr[...]) * g3_ref[...])


def _ffn(x2, ng, mod5, layer, w_gu, w_dn, *, seq, tm, tf):
    t, d = x2.shape
    ff = w_dn.shape[0]
    nf = ff // tf
    per_b = seq // tm
    mod_spec = lambda k: pl.BlockSpec((None, None, None, 1, d),
                                      lambda i, f: (layer, k, i // per_b, 0, 0))
    return pl.pallas_call(
        _ffn_kernel,
        grid=(t // tm, nf),
        in_specs=[pl.BlockSpec((tm, d), lambda i, f: (i, 0)),
                  pl.BlockSpec((None, 1, d), lambda i, f: (layer * 4 + 2, 0, 0)),
                  mod_spec(4), mod_spec(3),
                  pl.BlockSpec((d, tf), lambda i, f: (0, f)),
                  pl.BlockSpec((d, tf), lambda i, f: (0, nf + f)),
                  pl.BlockSpec((tf, d), lambda i, f: (f, 0)),
                  pl.BlockSpec((None, 1, d), lambda i, f: (layer * 4 + 3, 0, 0)),
                  mod_spec(5)],
        out_specs=pl.BlockSpec((tm, d), lambda i, f: (i, 0)),
        out_shape=jax.ShapeDtypeStruct((t, d), F32),
        scratch_shapes=[pltpu.VMEM((tm, d), BF16), pltpu.VMEM((tm, d), F32)],
        compiler_params=_cparams(("arbitrary", "arbitrary"), 56),
        name="ffn_swiglu",
    )(x2, ng, mod5, mod5, w_gu, w_gu, w_dn, ng, mod5)


def _route(h, wr, br, n_exp):
    logits = jnp.dot(h, wr, preferred_element_type=F32, precision=lax.Precision.HIGHEST) + br
    lane = lax.broadcasted_iota(jnp.int32, logits.shape, 1)
    logits = jnp.where(lane < n_exp, logits, -jnp.inf)
    ex = jnp.exp(logits - jnp.max(logits, axis=-1, keepdims=True))
    probs = ex / jnp.sum(ex, axis=-1, keepdims=True)
    p1 = jnp.max(probs, axis=-1, keepdims=True)
    i1 = jnp.min(jnp.where(probs == p1, lane, LANES), axis=-1, keepdims=True)
    rest = jnp.where(lane == i1, -1.0, probs)
    p2 = jnp.max(rest, axis=-1, keepdims=True)
    i2 = jnp.min(jnp.where(rest == p2, lane, LANES), axis=-1, keepdims=True)
    den = p1 + p2
    return jnp.where(lane == i1, p1 / den, jnp.where(lane == i2, p2 / den, 0.0))


def _moe_dense_kernel(x_ref, g2_ref, sc_ref, sh_ref, wr_ref, br_ref, wg_ref, wu_ref, wd_ref,
                      g3_ref, gate_ref, o_ref, h_scr, acc_scr, gates_scr, *, n_exp):
    e = pl.program_id(1)
    f = pl.program_id(2)

    @pl.when((e == 0) & (f == 0))
    def _():
        h = _norm_mod(x_ref[...], g2_ref[...], sc_ref[...], sh_ref[...])
        gates_scr[...] = _route(h, wr_ref[...], br_ref[...], n_exp)
        h_scr[...] = h.astype(BF16)
        acc_scr[...] = jnp.zeros_like(acc_scr)

    h = h_scr[...]
    gates = gates_scr[...]
    lane = lax.broadcasted_iota(jnp.int32, gates.shape, 1)
    ge = jnp.sum(jnp.where(lane == e, gates, 0.0), axis=-1, keepdims=True)
    gt = _bdot(h, wg_ref[...].astype(BF16))
    up = _bdot(h, wu_ref[...].astype(BF16))
    act = (_silu(gt) * up * ge).astype(BF16)
    acc_scr[...] += _bdot(act, wd_ref[...].astype(BF16))

    @pl.when((e == pl.num_programs(1) - 1) & (f == pl.num_programs(2) - 1))
    def _():
        o_ref[...] = x_ref[...] + gate_ref[...] * (_rms(acc_scr[...]) * g3_ref[...])


def _moe_dense(x2, ng, mod5, layer, w_router, b_router, w_gu, w_dn, *, seq, tm, tf):
    t, d = x2.shape
    n_exp, ff, _ = w_dn.shape
    nf = ff // tf
    per_b = seq // tm
    wr = jnp.zeros((d, LANES), F32).at[:, :n_exp].set(w_router.astype(F32))
    br = jnp.zeros((1, LANES), F32).at[0, :n_exp].set(b_router.astype(F32))
    mod_spec = lambda k: pl.BlockSpec((None, None, None, 1, d),
                                      lambda i, e, f: (layer, k, i // per_b, 0, 0))
    return pl.pallas_call(
        functools.partial(_moe_dense_kernel, n_exp=n_exp),
        grid=(t // tm, n_exp, nf),
        in_specs=[pl.BlockSpec((tm, d), lambda i, e, f: (i, 0)),
                  pl.BlockSpec((None, 1, d), lambda i, e, f: (layer * 4 + 2, 0, 0)),
                  mod_spec(4), mod_spec(3),
                  pl.BlockSpec((d, LANES), lambda i, e, f: (0, 0)),
                  pl.BlockSpec((1, LANES), lambda i, e, f: (0, 0)),
                  pl.BlockSpec((None, d, tf), lambda i, e, f: (e, 0, f)),
                  pl.BlockSpec((None, d, tf), lambda i, e, f: (e, 0, nf + f)),
                  pl.BlockSpec((None, tf, d), lambda i, e, f: (e, f, 0)),
                  pl.BlockSpec((None, 1, d), lambda i, e, f: (layer * 4 + 3, 0, 0)),
                  mod_spec(5)],
        out_specs=pl.BlockSpec((tm, d), lambda i, e, f: (i, 0)),
        out_shape=jax.ShapeDtypeStruct((t, d), F32),
        scratch_shapes=[pltpu.VMEM((tm, d), BF16), pltpu.VMEM((tm, d), F32),
                        pltpu.VMEM((tm, LANES), F32)],
        compiler_params=_cparams(("arbitrary", "arbitrary", "arbitrary"), 56),
        name="moe_swiglu",
    )(x2, ng, mod5, mod5, wr, br, w_gu, w_gu, w_dn, ng, mod5)


def _attn_kernel(q_ref, k_ref, v_ref, o_ref, *, tq, hd, scale):
    seq = q_ref.shape[0]
    row = lax.broadcasted_iota(jnp.int32, (tq, tq), 0)
    col = lax.broadcasted_iota(jnp.int32, (tq, tq), 1)
    strict = col < row
    after_mat = jnp.where(row > col, 1.0, 0.0).astype(BF16)
    lane = lax.broadcasted_iota(jnp.int32, (tq, 2 * hd), 1)
    first_head = lane < hd

    def kv_block(qh, k0, carry, acc, masked):
        kb = k_ref[pl.ds(k0, tq), :]
        vb = v_ref[pl.ds(k0, tq), :]
        z = lax.dot_general(qh, kb, (((1,), (1,)), ((), ())), preferred_element_type=F32)
        log_keep = -(jnp.maximum(z, 0.0) + jnp.log(1.0 + jnp.exp(-jnp.abs(z))))
        if masked:
            log_keep = jnp.where(strict, log_keep, 0.0)
        keep_after = _bdot(log_keep.astype(BF16), after_mat) + carry
        w = jnp.exp(z + log_keep + keep_after)
        if masked:
            w = jnp.where(strict, w, 0.0)
        acc = acc + _bdot(w.astype(BF16), vb)
        carry = carry + jnp.sum(log_keep, axis=-1, keepdims=True)
        return carry, acc

    def q_block(i, _):
        q0 = pl.multiple_of(i * tq, tq)
        q2 = q_ref[pl.ds(q0, tq), :] * scale
        accs = []
        for head in range(2):
            qh = jnp.where(first_head == (head == 0), q2, jnp.zeros_like(q2))
            carry, acc = kv_block(qh, q0, jnp.zeros((tq, 1), F32),
                                  jnp.zeros((tq, 2 * hd), F32), True)

            def body(jj, ca, qh=qh):
                k0 = pl.multiple_of((i - 1 - jj) * tq, tq)
                return kv_block(qh, k0, ca[0], ca[1], False)

            carry, acc = lax.fori_loop(0, i, body, (carry, acc))
            accs.append(acc)
        o_ref[pl.ds(q0, tq), :] = jnp.where(first_head, accs[0], accs[1]).astype(o_ref.dtype)
        return 0

    lax.fori_loop(0, seq // tq, q_block, 0)


def _attention(qkv, *, n_heads, hd, tq):
    bsz, seq, three_d = qkv.shape
    d = three_d // 3
    n_pair = n_heads // 2
    blk = lambda off: pl.BlockSpec((None, seq, 2 * hd), lambda b, p: (b, 0, off + p))
    return pl.pallas_call(
        functools.partial(_attn_kernel, tq=tq, hd=hd, scale=1.0 / math.sqrt(hd)),
        grid=(bsz, n_pair),
        in_specs=[blk(0), blk(n_pair), blk(2 * n_pair)],
        out_specs=pl.BlockSpec((None, seq, 2 * hd), lambda b, p: (b, 0, p)),
        out_shape=jax.ShapeDtypeStruct((bsz, seq, d), BF16),
        compiler_params=_cparams(("arbitrary", "arbitrary"), 32),
        name="stickbreaking_attention",
    )(qkv, qkv, qkv)


def kernel(x, c, ada_w, ada_b, norm_g, s5_w_in, s5_lambda_re, s5_lambda_im, s5_b_re, s5_b_im,
           s5_c_re, s5_c_im, s5_d, s5_log_step, s5_w_out, sb_w_qkv, sb_w_out, ffn_w_gate_up,
           ffn_w_down, moe_w_router, moe_b_router, moe_w_gate_up, moe_w_down):
    bsz, seq, d = x.shape
    depth = ada_w.shape[0]
    t = bsz * seq
    hd = 64
    n_heads = d // hd

    mod = _adaln(c, ada_w, ada_b)
    mod5 = mod.reshape(depth, 6, bsz, 1, d)
    ng = norm_g.reshape(depth * 4, 1, d)

    wd = s5_w_in.shape[2]
    u = _s5_in(x, ng, mod, s5_w_in[0].astype(BF16), tt=128)
    bbd, cbd, lam, dsk = _s5_discretise(s5_lambda_re[0], s5_lambda_im[0], s5_b_re[0], s5_b_im[0],
                                        s5_c_re[0], s5_c_im[0], s5_d[0], s5_log_step[0])
    y = _ssm(u.reshape(seq * bsz, wd), bbd, cbd, lam, dsk, nb=bsz, tc=128)
    x1 = _s5_out(y.reshape(seq, bsz * wd), s5_w_out[0].astype(BF16), x, ng, mod, tt=64)
    x2 = _ffn(x1.reshape(t, d), ng, mod5, 0, ffn_w_gate_up[0], ffn_w_down[0],
              seq=seq, tm=min(1024, seq), tf=512)

    qkv = _nm_mm(x2, ng, mod5, 1, sb_w_qkv[0].astype(BF16), seq=seq, tm=512, out_dtype=BF16)
    o = _attention(qkv.reshape(bsz, seq, 3 * d), n_heads=n_heads, hd=hd, tq=256)
    x3 = _mm_post(o.reshape(t, d), sb_w_out[0].astype(BF16), x2, ng, mod5, 1, seq=seq, tm=512)
    x4 = _moe_dense(x3, ng, mod5, 1, moe_w_router[0], moe_b_router[0], moe_w_gate_up[0],
                    moe_w_down[0], seq=seq, tm=min(1024, seq), tf=512)
    return x4.reshape(bsz, seq, d)
```

```python
import functools
import math

import jax
import jax.numpy as jnp
from jax import lax
from jax.experimental import pallas as pl
from jax.experimental.pallas import tpu as pltpu

F32 = jnp.float32
BF16 = jnp.bfloat16

NORM_EPS = 1e-6
LANES = 128
MIB = 1024 * 1024


def _cparams(semantics, vmem_mib):
    return pltpu.CompilerParams(dimension_semantics=semantics,
                                vmem_limit_bytes=vmem_mib * MIB)


def _rms(x):
    return x * lax.rsqrt(jnp.mean(x * x, axis=-1, keepdims=True) + NORM_EPS)


def _norm_mod(x, g, scale, shift):
    return (_rms(x) * g) * (1.0 + scale) + shift


def _silu(x):
    return x * jax.nn.sigmoid(x)


def _bdot(a, b):
    return jnp.dot(a, b, preferred_element_type=F32)


def _adaln_kernel(c_ref, w_ref, b_ref, o_ref):
    ca = _silu(c_ref[...]).astype(BF16)
    o_ref[...] = _bdot(ca, w_ref[...].astype(BF16)) + b_ref[...]


def _adaln(c, ada_w, ada_b):
    depth, d, six_d = ada_w.shape
    bsz = c.shape[0]
    n6 = six_d // d
    return pl.pallas_call(
        _adaln_kernel,
        grid=(depth, n6),
        in_specs=[pl.BlockSpec((bsz, d), lambda l, k: (0, 0)),
                  pl.BlockSpec((None, d, d), lambda l, k: (l, 0, k)),
                  pl.BlockSpec((None, None, 1, d), lambda l, k: (l, k, 0, 0))],
        out_specs=pl.BlockSpec((None, None, bsz, d), lambda l, k: (l, k, 0, 0)),
        out_shape=jax.ShapeDtypeStruct((depth, n6, bsz, d), F32),
        compiler_params=_cparams(("arbitrary", "arbitrary"), 24),
        name="adaln",
    )(c, ada_w, ada_b.reshape(depth, n6, 1, d))


def _nm_mm_kernel(x_ref, g_ref, sc_ref, sh_ref, w_ref, o_ref, *, n_chunk):
    h = _norm_mod(x_ref[...], g_ref[...], sc_ref[...], sh_ref[...]).astype(BF16)
    for n0 in range(0, o_ref.shape[1], n_chunk):
        o_ref[:, n0:n0 + n_chunk] = _bdot(h, w_ref[:, n0:n0 + n_chunk]).astype(o_ref.dtype)


def _nm_mm(x2, ng, mod5, layer, w, *, seq, tm, out_dtype):
    t, d = x2.shape
    n = w.shape[1]
    per_b = seq // tm
    mod_spec = lambda k: pl.BlockSpec((None, None, None, 1, d),
                                      lambda i: (layer, k, i // per_b, 0, 0))
    return pl.pallas_call(
        functools.partial(_nm_mm_kernel, n_chunk=min(n, 1024)),
        grid=(t // tm,),
        in_specs=[pl.BlockSpec((tm, d), lambda i: (i, 0)),
                  pl.BlockSpec((None, 1, d), lambda i: (layer * 4, 0, 0)),
                  mod_spec(1), mod_spec(0),
                  pl.BlockSpec((d, n), lambda i: (0, 0))],
        out_specs=pl.BlockSpec((tm, n), lambda i: (i, 0)),
        out_shape=jax.ShapeDtypeStruct((t, n), out_dtype),
        compiler_params=_cparams(("arbitrary",), 48),
        name="norm_mod_matmul",
    )(x2, ng, mod5, mod5, w)


def _s5_in_kernel(x_ref, g_ref, sc_ref, sh_ref, w_ref, o_ref):
    nb = x_ref.shape[0]
    wd = w_ref.shape[1]
    for b in range(nb):
        h = _norm_mod(x_ref[b], g_ref[...], sc_ref[b:b + 1, :], sh_ref[b:b + 1, :])
        o_ref[:, b * wd:(b + 1) * wd] = _bdot(h.astype(BF16), w_ref[...])


def _s5_in(x, ng, mod, w_in, *, tt):
    bsz, seq, d = x.shape
    wd = w_in.shape[1]
    mod_spec = lambda k: pl.BlockSpec((None, None, bsz, d), lambda i: (0, k, 0, 0))
    return pl.pallas_call(
        _s5_in_kernel,
        grid=(seq // tt,),
        in_specs=[pl.BlockSpec((bsz, tt, d), lambda i: (0, i, 0)),
                  pl.BlockSpec((None, 1, d), lambda i: (0, 0, 0)),
                  mod_spec(1), mod_spec(0),
                  pl.BlockSpec((d, wd), lambda i: (0, 0))],
        out_specs=pl.BlockSpec((tt, bsz * wd), lambda i: (i, 0)),
        out_shape=jax.ShapeDtypeStruct((seq, bsz * wd), F32),
        compiler_params=_cparams(("arbitrary",), 40),
        name="s5_in",
    )(x, ng, mod, mod, w_in)


def _ssm_kernel(u_ref, bbd_ref, cbd_ref, lam_ref, d_ref, y_ref, bs_ref, st_ref, *, nb, tc):
    n_slab = bbd_ref.shape[0]
    half = bbd_ref.shape[2] // 2

    @pl.when(pl.program_id(0) == 0)
    def _():
        st_ref[...] = jnp.zeros_like(st_ref)

    for j in range(n_slab):
        lanes = slice(j * LANES, (j + 1) * LANES)
        uj = u_ref[:, lanes]
        bs_ref[...] = _bdot(uj.astype(BF16), bbd_ref[j])
        lr = jnp.broadcast_to(lam_ref[j, 0:1, :], (nb, half))
        li = jnp.broadcast_to(lam_ref[j, 1:2, :], (nb, half))

        def step(t, carry):
            sr, si = carry
            r0 = pl.multiple_of(t * nb, nb)
            rows = pl.ds(r0, nb)
            nr = lr * sr - li * si + bs_ref[rows, :half]
            ni = lr * si + li * sr + bs_ref[rows, half:]
            bs_ref[rows, :half] = nr
            bs_ref[rows, half:] = ni
            return nr, ni

        sr, si = lax.fori_loop(0, tc, step, (st_ref[j, :, :half], st_ref[j, :, half:]),
                               unroll=4)
        st_ref[j, :, :half] = sr
        st_ref[j, :, half:] = si
        yj = _bdot(bs_ref[...].astype(BF16), cbd_ref[j]) + d_ref[:, lanes] * uj
        y_ref[:, lanes] = jax.nn.gelu(yj).astype(y_ref.dtype)


def _ssm(u2, bbd, cbd, lam, d_skip, *, nb, tc):
    rows, wd = u2.shape
    n_slab, _, two_p = bbd.shape
    tr = tc * nb
    return pl.pallas_call(
        functools.partial(_ssm_kernel, nb=nb, tc=tc),
        grid=(rows // tr,),
        in_specs=[pl.BlockSpec((tr, wd), lambda i: (i, 0)),
                  pl.BlockSpec(bbd.shape, lambda i: (0, 0, 0)),
                  pl.BlockSpec(cbd.shape, lambda i: (0, 0, 0)),
                  pl.BlockSpec(lam.shape, lambda i: (0, 0, 0)),
                  pl.BlockSpec((1, wd), lambda i: (0, 0))],
        out_specs=pl.BlockSpec((tr, wd), lambda i: (i, 0)),
        out_shape=jax.ShapeDtypeStruct((rows, wd), BF16),
        scratch_shapes=[pltpu.VMEM((tr, two_p), F32),
                        pltpu.VMEM((n_slab, nb, two_p), F32)],
        compiler_params=_cparams(("arbitrary",), 48),
        name="s5_scan",
    )(u2, bbd, cbd, lam, d_skip)


def _s5_discretise(lam_re, lam_im, b_re, b_im, c_re, c_im, d_skip, log_step):
    g, p, hh = b_re.shape
    gs = LANES // hh
    n_slab = g // gs
    step = jnp.exp(log_step.astype(F32))[:, None]
    lam_re, lam_im = lam_re.astype(F32), lam_im.astype(F32)
    mag = jnp.exp(lam_re * step)
    lb_re = mag * jnp.cos(lam_im * step)
    lb_im = mag * jnp.sin(lam_im * step)
    den = lam_re * lam_re + lam_im * lam_im
    coef_re = ((lb_re - 1.0) * lam_re + lb_im * lam_im) / den
    coef_im = (lb_im * lam_re - (lb_re - 1.0) * lam_im) / den
    bb_re = coef_re[..., None] * b_re - coef_im[..., None] * b_im
    bb_im = coef_re[..., None] * b_im + coef_im[..., None] * b_re
    eye = jnp.eye(gs, dtype=F32)

    def blockdiag_in(bb):
        t = bb.reshape(n_slab, gs, p, hh).transpose(0, 1, 3, 2)
        return jnp.einsum("jghp,gk->jghkp", t, eye).reshape(n_slab, gs * hh, gs * p)

    def blockdiag_out(cc):
        t = cc.reshape(n_slab, gs, hh, p).transpose(0, 1, 3, 2)
        return jnp.einsum("jgph,gk->jgpkh", t, eye).reshape(n_slab, gs * p, gs * hh)

    bbd = jnp.concatenate([blockdiag_in(bb_re), blockdiag_in(bb_im)], axis=-1).astype(BF16)
    cbd = jnp.concatenate([blockdiag_out(c_re.astype(F32)),
                           -blockdiag_out(c_im.astype(F32))], axis=1).astype(BF16)
    lam = jnp.stack([lb_re.reshape(n_slab, gs * p), lb_im.reshape(n_slab, gs * p)], axis=1)
    return bbd, cbd, lam, d_skip.astype(F32).reshape(1, g * hh)


def _s5_out_kernel(y_ref, w_ref, x_ref, g_ref, gate_ref, o_ref):
    nb = x_ref.shape[0]
    wd = w_ref.shape[0]
    d = x_ref.shape[2]
    for b in range(nb):
        z = _bdot(y_ref[:, b * wd:(b + 1) * wd], w_ref[...])
        v = z[:, :d] * jax.nn.sigmoid(z[:, d:])
        o_ref[b] = x_ref[b] + gate_ref[b:b + 1, :] * (_rms(v) * g_ref[...])


def _s5_out(y, w_out, x, ng, mod, *, tt):
    bsz, seq, d = x.shape
    wd = w_out.shape[0]
    return pl.pallas_call(
        _s5_out_kernel,
        grid=(seq // tt,),
        in_specs=[pl.BlockSpec((tt, bsz * wd), lambda i: (i, 0)),
                  pl.BlockSpec(w_out.shape, lambda i: (0, 0)),
                  pl.BlockSpec((bsz, tt, d), lambda i: (0, i, 0)),
                  pl.BlockSpec((None, 1, d), lambda i: (1, 0, 0)),
                  pl.BlockSpec((None, None, bsz, d), lambda i: (0, 2, 0, 0))],
        out_specs=pl.BlockSpec((bsz, tt, d), lambda i: (0, i, 0)),
        out_shape=jax.ShapeDtypeStruct(x.shape, F32),
        compiler_params=_cparams(("arbitrary",), 48),
        name="s5_out",
    )(y, w_out, x, ng, mod)


def _mm_post_kernel(a_ref, w_ref, x_ref, g_ref, gate_ref, o_ref):
    y = _bdot(a_ref[...], w_ref[...])
    o_ref[...] = x_ref[...] + gate_ref[...] * (_rms(y) * g_ref[...])


def _mm_post(a, w, x2, ng, mod5, layer, *, seq, tm):
    t, d = x2.shape
    k = a.shape[1]
    per_b = seq // tm
    return pl.pallas_call(
        _mm_post_kernel,
        grid=(t // tm,),
        in_specs=[pl.BlockSpec((tm, k), lambda i: (i, 0)),
                  pl.BlockSpec(w.shape, lambda i: (0, 0)),
                  pl.BlockSpec((tm, d), lambda i: (i, 0)),
                  pl.BlockSpec((None, 1, d), lambda i: (layer * 4 + 1, 0, 0)),
                  pl.BlockSpec((None, None, None, 1, d),
                               lambda i: (layer, 2, i // per_b, 0, 0))],
        out_specs=pl.BlockSpec((tm, d), lambda i: (i, 0)),
        out_shape=jax.ShapeDtypeStruct((t, d), F32),
        compiler_params=_cparams(("arbitrary",), 40),
        name="matmul_post",
    )(a, w, x2, ng, mod5)


def _ffn_kernel(x_ref, g2_ref, sc_ref, sh_ref, wg_ref, wu_ref, wd_ref, g3_ref, gate_ref,
                o_ref, h_scr, acc_scr):
    f = pl.program_id(1)

    @pl.when(f == 0)
    def _():
        h_scr[...] = _norm_mod(x_ref[...], g2_ref[...], sc_ref[...], sh_ref[...]).astype(BF16)
        acc_scr[...] = jnp.zeros_like(acc_scr)

    h = h_scr[...]
    gt = _bdot(h, wg_ref[...].astype(BF16))
    up = _bdot(h, wu_ref[...].astype(BF16))
    act = (_silu(gt) * up).astype(BF16)
    acc_scr[...] += _bdot(act, wd_ref[...].astype(BF16))

    @pl.when(f == pl.num_programs(1) - 1)
    def _():
        o_ref[...] = x_ref[...] + gate_ref[...] * (_rms(acc_scr[...]) * g3_ref[...])


def _ffn(x2, ng, mod5, layer, w_gu, w_dn, *, seq, tm, tf):
    t, d = x2.shape
    ff = w_dn.shape[0]
    nf = ff // tf
    per_b = seq // tm
    mod_spec = lambda k: pl.BlockSpec((None, None, None, 1, d),
                                      lambda i, f: (layer, k, i // per_b, 0, 0))
    return pl.pallas_call(
        _ffn_kernel,
        grid=(t // tm, nf),
        in_specs=[pl.BlockSpec((tm, d), lambda i, f: (i, 0)),
                  pl.BlockSpec((None, 1, d), lambda i, f: (layer * 4 + 2, 0, 0)),
                  mod_spec(4), mod_spec(3),
                  pl.BlockSpec((d, tf), lambda i, f: (0, f)),
                  pl.BlockSpec((d, tf), lambda i, f: (0, nf + f)),
                  pl.BlockSpec((tf, d), lambda i, f: (f, 0)),
                  pl.BlockSpec((None, 1, d), lambda i, f: (layer * 4 + 3, 0, 0)),
                  mod_spec(5)],
        out_specs=pl.BlockSpec((tm, d), lambda i, f: (i, 0)),
        out_shape=jax.ShapeDtypeStruct((t, d), F32),
        scratch_shapes=[pltpu.VMEM((tm, d), BF16), pltpu.VMEM((tm, d), F32)],
        compiler_params=_cparams(("arbitrary", "arbitrary"), 56),
        name="ffn_swiglu",
    )(x2, ng, mod5, mod5, w_gu, w_gu, w_dn, ng, mod5)


def _router_kernel(x_ref, g2_ref, sc_ref, sh_ref, wr_ref, br_ref,
                   h_ref, gsel_ref, ridx_ref, cnt_ref, run_scr, *, n_exp):
    @pl.when(pl.program_id(0) == 0)
    def _():
        run_scr[...] = jnp.zeros_like(run_scr)

    h = _norm_mod(x_ref[...], g2_ref[...], sc_ref[...], sh_ref[...])
    h_ref[...] = h
    logits = jnp.dot(h, wr_ref[...], preferred_element_type=F32,
                     precision=lax.Precision.HIGHEST) + br_ref[...]
    tm = logits.shape[0]
    lane = lax.broadcasted_iota(jnp.int32, logits.shape, 1)
    logits = jnp.where(lane < n_exp, logits, -jnp.inf)
    ex = jnp.exp(logits - jnp.max(logits, axis=-1, keepdims=True))
    probs = ex / jnp.sum(ex, axis=-1, keepdims=True)
    p1 = jnp.max(probs, axis=-1, keepdims=True)
    i1 = jnp.min(jnp.where(probs == p1, lane, LANES), axis=-1, keepdims=True)
    rest = jnp.where(lane == i1, -1.0, probs)
    p2 = jnp.max(rest, axis=-1, keepdims=True)
    i2 = jnp.min(jnp.where(rest == p2, lane, LANES), axis=-1, keepdims=True)
    den = p1 + p2
    gsel_ref[...] = jnp.where(lane == 0, p1 / den, jnp.where(lane == 1, p2 / den, 0.0))

    sel = jnp.where((lane == i1) | (lane == i2), 1.0, 0.0)
    row = lax.broadcasted_iota(jnp.int32, (tm, tm), 0)
    col = lax.broadcasted_iota(jnp.int32, (tm, tm), 1)
    earlier = jnp.where(col < row, 1.0, 0.0).astype(BF16)
    rank = _bdot(earlier, sel.astype(BF16)) + run_scr[0:1, :]
    r1 = jnp.sum(jnp.where(lane == i1, rank, 0.0), axis=-1, keepdims=True).astype(jnp.int32)
    r2 = jnp.sum(jnp.where(lane == i2, rank, 0.0), axis=-1, keepdims=True).astype(jnp.int32)
    ridx_ref[...] = jnp.where(lane == 0, i1, jnp.where(lane == 1, i2,
                              jnp.where(lane == 2, r1, jnp.where(lane == 3, r2, 0))))
    run_scr[...] = run_scr[...] + jnp.sum(sel, axis=0, keepdims=True)
    cnt_ref[...] = run_scr[...]


def _router(x2, ng, mod5, layer, w_router, b_router, *, seq, tm):
    t, d = x2.shape
    n_exp = w_router.shape[1]
    per_b = seq // tm
    wr = jnp.zeros((d, LANES), F32).at[:, :n_exp].set(w_router.astype(F32))
    br = jnp.zeros((1, LANES), F32).at[0, :n_exp].set(b_router.astype(F32))
    mod_spec = lambda k: pl.BlockSpec((None, None, None, 1, d),
                                      lambda i: (layer, k, i // per_b, 0, 0))
    row_blk = lambda w: pl.BlockSpec((tm, w), lambda i: (i, 0))
    return pl.pallas_call(
        functools.partial(_router_kernel, n_exp=n_exp),
        grid=(t // tm,),
        in_specs=[row_blk(d),
                  pl.BlockSpec((None, 1, d), lambda i: (layer * 4 + 2, 0, 0)),
                  mod_spec(4), mod_spec(3),
                  pl.BlockSpec((d, LANES), lambda i: (0, 0)),
                  pl.BlockSpec((1, LANES), lambda i: (0, 0))],
        out_specs=[row_blk(d), row_blk(LANES), row_blk(LANES),
                   pl.BlockSpec((8, LANES), lambda i: (0, 0))],
        out_shape=[jax.ShapeDtypeStruct((t, d), F32),
                   jax.ShapeDtypeStruct((t, LANES), F32),
                   jax.ShapeDtypeStruct((t, LANES), jnp.int32),
                   jax.ShapeDtypeStruct((8, LANES), F32)],
        scratch_shapes=[pltpu.VMEM((8, LANES), F32)],
        compiler_params=_cparams(("arbitrary",), 32),
        name="moe_router",
    )(x2, ng, mod5, mod5, wr, br)


def _row_copy(src_ref, src_row, dst_ref, dst_row, sem):
    return pltpu.make_async_copy(src_ref.at[pl.ds(src_row, 1)], dst_ref.at[pl.ds(dst_row, 1)], sem)


def _dispatch_kernel(pa_ref, pb_ref, h_ref, hs_in_ref, hs_ref, sem):
    del hs_in_ref
    n = h_ref.shape[0]

    def issue(r, _):
        _row_copy(h_ref, r, hs_ref, pa_ref[r], sem).start()
        _row_copy(h_ref, r, hs_ref, pb_ref[r], sem).start()
        return 0

    def drain(r, _):
        _row_copy(h_ref, r, hs_ref, pa_ref[r], sem).wait()
        _row_copy(h_ref, r, hs_ref, pb_ref[r], sem).wait()
        return 0

    lax.fori_loop(0, n, issue, 0, unroll=8)
    lax.fori_loop(0, n, drain, 0, unroll=8)


def _dispatch(h, pos_a, pos_b, hs_zero, *, tm):
    t, d = h.shape
    smem_blk = pl.BlockSpec((tm,), lambda i: (i,), memory_space=pltpu.SMEM)
    return pl.pallas_call(
        _dispatch_kernel,
        grid=(t // tm,),
        in_specs=[smem_blk, smem_blk,
                  pl.BlockSpec((tm, d), lambda i: (i, 0)),
                  pl.BlockSpec(memory_space=pl.ANY)],
        out_specs=pl.BlockSpec(memory_space=pl.ANY),
        out_shape=jax.ShapeDtypeStruct(hs_zero.shape, hs_zero.dtype),
        scratch_shapes=[pltpu.SemaphoreType.DMA],
        input_output_aliases={3: 0},
        compiler_params=_cparams(("arbitrary",), 24),
        name="moe_dispatch",
    )(pos_a, pos_b, h, hs_zero)


def _expert_kernel(te_ref, nu_ref, hs_ref, wg_ref, wu_ref, wd_ref, y_ref, h_scr, acc_scr):
    del te_ref
    f = pl.program_id(1)
    last = pl.num_programs(1) - 1
    used = pl.program_id(0) < nu_ref[0]

    @pl.when(jnp.logical_not(used) & (f == last))
    def _():
        y_ref[...] = jnp.zeros_like(y_ref)

    @pl.when(used)
    def _():
        @pl.when(f == 0)
        def _():
            h_scr[...] = hs_ref[...].astype(BF16)
            acc_scr[...] = jnp.zeros_like(acc_scr)

        h = h_scr[...]
        gt = _bdot(h, wg_ref[...].astype(BF16))
        up = _bdot(h, wu_ref[...].astype(BF16))
        act = (_silu(gt) * up).astype(BF16)
        acc_scr[...] += _bdot(act, wd_ref[...].astype(BF16))

        @pl.when(f == last)
        def _():
            y_ref[...] = acc_scr[...]


def _experts(hs, tile_expert, n_used, w_gu, w_dn, *, tm, tf):
    npad, d = hs.shape
    n_exp, ff, _ = w_dn.shape
    nf = ff // tf
    row = lambda i, f, te, nu: (jnp.minimum(i, nu[0] - 1), 0)
    fe = lambda i, f, nu: jnp.where(i < nu[0], f, nf - 1)
    return pl.pallas_call(
        _expert_kernel,
        grid_spec=pltpu.PrefetchScalarGridSpec(
            num_scalar_prefetch=2,
            grid=(npad // tm, nf),
            in_specs=[pl.BlockSpec((tm, d), row),
                      pl.BlockSpec((None, d, tf), lambda i, f, te, nu: (te[i], 0, fe(i, f, nu))),
                      pl.BlockSpec((None, d, tf),
                                   lambda i, f, te, nu: (te[i], 0, nf + fe(i, f, nu))),
                      pl.BlockSpec((None, tf, d), lambda i, f, te, nu: (te[i], fe(i, f, nu), 0))],
            out_specs=pl.BlockSpec((tm, d), lambda i, f, te, nu: (i, 0)),
            scratch_shapes=[pltpu.VMEM((tm, d), BF16), pltpu.VMEM((tm, d), F32)]),
        out_shape=jax.ShapeDtypeStruct((npad, d), F32),
        compiler_params=_cparams(("arbitrary", "arbitrary"), 56),
        name="moe_experts",
    )(tile_expert, n_used, hs, w_gu, w_gu, w_dn)


def _combine_kernel(pa_ref, pb_ref, x_ref, gsel_ref, g3_ref, gate_ref, ys_ref, o_ref,
                    ya_scr, yb_scr, sem):
    n = x_ref.shape[0]

    def issue(r, _):
        _row_copy(ys_ref, pa_ref[r], ya_scr, r, sem).start()
        _row_copy(ys_ref, pb_ref[r], yb_scr, r, sem).start()
        return 0

    def drain(r, _):
        _row_copy(ys_ref, pa_ref[r], ya_scr, r, sem).wait()
        _row_copy(ys_ref, pb_ref[r], yb_scr, r, sem).wait()
        return 0

    lax.fori_loop(0, n, issue, 0, unroll=8)
    lax.fori_loop(0, n, drain, 0, unroll=8)
    gs = gsel_ref[...]
    y = gs[:, 0:1] * ya_scr[...] + gs[:, 1:2] * yb_scr[...]
    o_ref[...] = x_ref[...] + gate_ref[...] * (_rms(y) * g3_ref[...])


def _combine(x2, gsel, pos_a, pos_b, ys, ng, mod5, layer, *, seq, tm):
    t, d = x2.shape
    per_b = seq // tm
    smem_blk = pl.BlockSpec((tm,), lambda i: (i,), memory_space=pltpu.SMEM)
    return pl.pallas_call(
        _combine_kernel,
        grid=(t // tm,),
        in_specs=[smem_blk, smem_blk,
                  pl.BlockSpec((tm, d), lambda i: (i, 0)),
                  pl.BlockSpec((tm, LANES), lambda i: (i, 0)),
                  pl.BlockSpec((None, 1, d), lambda i: (layer * 4 + 3, 0, 0)),
                  pl.BlockSpec((None, None, None, 1, d),
                               lambda i: (layer, 5, i // per_b, 0, 0)),
                  pl.BlockSpec(memory_space=pl.ANY)],
        out_specs=pl.BlockSpec((tm, d), lambda i: (i, 0)),
        out_shape=jax.ShapeDtypeStruct((t, d), F32),
        scratch_shapes=[pltpu.VMEM((tm, d), F32), pltpu.VMEM((tm, d), F32),
                        pltpu.SemaphoreType.DMA],
        compiler_params=_cparams(("arbitrary",), 40),
        name="moe_combine",
    )(pos_a, pos_b, x2, gsel, ng, mod5, ys)


def _moe(x2, ng, mod5, layer, w_router, b_router, w_gu, w_dn, *, seq, tm_tok, tm_exp, tf):
    t, d = x2.shape
    n_exp = w_router.shape[1]
    h, gsel, ridx, cnt = _router(x2, ng, mod5, layer, w_router, b_router, seq=seq, tm=tm_tok)

    counts = cnt[0, :n_exp].astype(jnp.int32)
    padded = ((counts + tm_exp - 1) // tm_exp) * tm_exp
    ends = jnp.cumsum(padded)
    starts = ends - padded
    pos_a = starts[ridx[:, 0]] + ridx[:, 2]
    pos_b = starts[ridx[:, 1]] + ridx[:, 3]
    n_tiles = (2 * t + n_exp * (tm_exp - 1)) // tm_exp
    n_used = (ends[-1] // tm_exp).reshape(1)
    tile_row = jnp.arange(n_tiles, dtype=jnp.int32) * tm_exp
    tile_expert = jnp.sum(tile_row[:, None] >= ends[None, :], axis=1).astype(jnp.int32)
    tile_expert = jnp.where(jnp.arange(n_tiles) < n_used[0], tile_expert,
                            tile_expert[n_used[0] - 1])

    hs = _dispatch(h, pos_a, pos_b, jnp.zeros((n_tiles * tm_exp, d), F32), tm=tm_tok)
    ys = _experts(hs, tile_expert, n_used, w_gu, w_dn, tm=tm_exp, tf=tf)
    return _combine(x2, gsel, pos_a, pos_b, ys, ng, mod5, layer, seq=seq, tm=tm_tok)


def _attn_kernel(q_ref, k_ref, v_ref, o_ref, acc_scr, *, tq, hd, pairs, scale):
    seq = q_ref.shape[0]
    row = lax.broadcasted_iota(jnp.int32, (tq, tq), 0)
    col = lax.broadcasted_iota(jnp.int32, (tq, tq), 1)
    strict = col < row
    after_mat = jnp.where(row > col, 1.0, 0.0).astype(BF16)
    lane = lax.broadcasted_iota(jnp.int32, (tq, LANES), 1)
    first_head = lane < hd
    heads = [(p, h) for p in range(pairs) for h in range(2)]

    def kv_block(qhs, k0, drops, masked):
        new_drops = []
        for n, (p, _) in enumerate(heads):
            lanes = slice(p * LANES, (p + 1) * LANES)
            z = lax.dot_general(qhs[n], k_ref[pl.ds(k0, tq), lanes], (((1,), (1,)), ((), ())),
                                preferred_element_type=F32)
            neg_abs = pltpu.bitcast(pltpu.bitcast(z, jnp.uint32) | jnp.uint32(0x80000000), F32)
            sp = jnp.maximum(z, 0.0) + jnp.log(1.0 + jnp.exp(neg_abs))
            if masked:
                sp = jnp.where(strict, sp, 0.0)
            drop_after = _bdot(sp.astype(BF16), after_mat) + drops[n]
            w = jnp.exp((z - sp) - drop_after)
            if masked:
                w = jnp.where(strict, w, 0.0)
            acc_scr[n] += _bdot(w.astype(BF16), v_ref[pl.ds(k0, tq), lanes])
            new_drops.append(drops[n] + jnp.sum(sp, axis=-1, keepdims=True))
        return tuple(new_drops)

    def q_block(i, _):
        q0 = pl.multiple_of(i * tq, tq)
        qhs = []
        for p, h in heads:
            q2 = q_ref[pl.ds(q0, tq), p * LANES:(p + 1) * LANES] * scale
            qhs.append(jnp.where(first_head == (h == 0), q2, jnp.zeros_like(q2)))
        acc_scr[...] = jnp.zeros_like(acc_scr)
        zero = jnp.zeros((tq, 1), F32)
        drops = kv_block(qhs, q0, (zero,) * len(heads), True)

        def body(jj, drops):
            k0 = pl.multiple_of((i - 1 - jj) * tq, tq)
            return kv_block(qhs, k0, drops, False)

        lax.fori_loop(0, i, body, drops)
        for p in range(pairs):
            o_ref[pl.ds(q0, tq), p * LANES:(p + 1) * LANES] = jnp.where(
                first_head, acc_scr[2 * p], acc_scr[2 * p + 1]).astype(o_ref.dtype)
        return 0

    lax.fori_loop(0, seq // tq, q_block, 0)


def _attention(qkv, *, hd, tq, pairs):
    bsz, seq, three_d = qkv.shape
    d = three_d // 3
    wl = pairs * LANES
    n_step = d // wl
    blk = lambda off: pl.BlockSpec((None, seq, wl), lambda b, p: (b, 0, off + p))
    return pl.pallas_call(
        functools.partial(_attn_kernel, tq=tq, hd=hd, pairs=pairs, scale=1.0 / math.sqrt(hd)),
        grid=(bsz, n_step),
        in_specs=[blk(0), blk(n_step), blk(2 * n_step)],
        out_specs=pl.BlockSpec((None, seq, wl), lambda b, p: (b, 0, p)),
        out_shape=jax.ShapeDtypeStruct((bsz, seq, d), BF16),
        scratch_shapes=[pltpu.VMEM((2 * pairs, tq, LANES), F32)],
        compiler_params=_cparams(("arbitrary", "arbitrary"), 32),
        name="stickbreaking_attention",
    )(qkv, qkv, qkv)


def kernel(x, c, ada_w, ada_b, norm_g, s5_w_in, s5_lambda_re, s5_lambda_im, s5_b_re, s5_b_im,
           s5_c_re, s5_c_im, s5_d, s5_log_step, s5_w_out, sb_w_qkv, sb_w_out, ffn_w_gate_up,
           ffn_w_down, moe_w_router, moe_b_router, moe_w_gate_up, moe_w_down):
    bsz, seq, d = x.shape
    depth = ada_w.shape[0]
    t = bsz * seq
    hd = 64
    n_heads = d // hd

    mod = _adaln(c, ada_w, ada_b)
    mod5 = mod.reshape(depth, 6, bsz, 1, d)
    ng = norm_g.reshape(depth * 4, 1, d)

    wd = s5_w_in.shape[2]
    u = _s5_in(x, ng, mod, s5_w_in[0].astype(BF16), tt=128)
    bbd, cbd, lam, dsk = _s5_discretise(s5_lambda_re[0], s5_lambda_im[0], s5_b_re[0], s5_b_im[0],
                                        s5_c_re[0], s5_c_im[0], s5_d[0], s5_log_step[0])
    y = _ssm(u.reshape(seq * bsz, wd), bbd, cbd, lam, dsk, nb=bsz, tc=128)
    x1 = _s5_out(y.reshape(seq, bsz * wd), s5_w_out[0].astype(BF16), x, ng, mod, tt=64)
    x2 = _ffn(x1.reshape(t, d), ng, mod5, 0, ffn_w_gate_up[0], ffn_w_down[0],
              seq=seq, tm=min(1024, seq), tf=512)

    qkv = _nm_mm(x2, ng, mod5, 1, sb_w_qkv[0].astype(BF16), seq=seq, tm=512, out_dtype=BF16)
    o = _attention(qkv.reshape(bsz, seq, 3 * d), hd=hd, tq=256, pairs=2)
    x3 = _mm_post(o.reshape(t, d), sb_w_out[0].astype(BF16), x2, ng, mod5, 1, seq=seq, tm=512)
    x4 = _moe(x3, ng, mod5, 1, moe_w_router[0], moe_b_router[0], moe_w_gate_up[0],
              moe_w_down[0], seq=seq, tm_tok=min(1024, seq), tm_exp=1024, tf=512)
    return x4.reshape(bsz, seq, d)
```

```python
import functools
import math

import jax
import jax.numpy as jnp
from jax import lax
from jax.experimental import pallas as pl
from jax.experimental.pallas import tpu as pltpu

F32 = jnp.float32
BF16 = jnp.bfloat16

NORM_EPS = 1e-6
DROP_DEAD = 104.0
LANES = 128
MIB = 1024 * 1024


def _cparams(semantics, vmem_mib):
    return pltpu.CompilerParams(dimension_semantics=semantics,
                                vmem_limit_bytes=vmem_mib * MIB)


def _rms(x):
    return x * lax.rsqrt(jnp.mean(x * x, axis=-1, keepdims=True) + NORM_EPS)


def _norm_mod(x, g, scale, shift):
    return (_rms(x) * g) * (1.0 + scale) + shift


def _silu(x):
    return x * jax.nn.sigmoid(x)


def _bdot(a, b):
    return jnp.dot(a, b, preferred_element_type=F32)


def _adaln_kernel(c_ref, w_ref, b_ref, o_ref):
    ca = _silu(c_ref[...]).astype(BF16)
    o_ref[...] = _bdot(ca, w_ref[...].astype(BF16)) + b_ref[...]


def _adaln(c, ada_w, ada_b):
    depth, d, six_d = ada_w.shape
    bsz = c.shape[0]
    n6 = six_d // d
    return pl.pallas_call(
        _adaln_kernel,
        grid=(depth, n6),
        in_specs=[pl.BlockSpec((bsz, d), lambda l, k: (0, 0)),
                  pl.BlockSpec((None, d, d), lambda l, k: (l, 0, k)),
                  pl.BlockSpec((None, None, 1, d), lambda l, k: (l, k, 0, 0))],
        out_specs=pl.BlockSpec((None, None, bsz, d), lambda l, k: (l, k, 0, 0)),
        out_shape=jax.ShapeDtypeStruct((depth, n6, bsz, d), F32),
        compiler_params=_cparams(("arbitrary", "arbitrary"), 24),
        name="adaln",
    )(c, ada_w, ada_b.reshape(depth, n6, 1, d))


def _nm_mm_kernel(x_ref, g_ref, sc_ref, sh_ref, w_ref, o_ref, *, n_chunk):
    h = _norm_mod(x_ref[...], g_ref[...], sc_ref[...], sh_ref[...]).astype(BF16)
    for n0 in range(0, o_ref.shape[1], n_chunk):
        o_ref[:, n0:n0 + n_chunk] = _bdot(h, w_ref[:, n0:n0 + n_chunk]).astype(o_ref.dtype)


def _nm_mm(x2, ng, mod5, layer, w, *, seq, tm, out_dtype):
    t, d = x2.shape
    n = w.shape[1]
    per_b = seq // tm
    mod_spec = lambda k: pl.BlockSpec((None, None, None, 1, d),
                                      lambda i: (layer, k, i // per_b, 0, 0))
    return pl.pallas_call(
        functools.partial(_nm_mm_kernel, n_chunk=min(n, 1024)),
        grid=(t // tm,),
        in_specs=[pl.BlockSpec((tm, d), lambda i: (i, 0)),
                  pl.BlockSpec((None, 1, d), lambda i: (layer * 4, 0, 0)),
                  mod_spec(1), mod_spec(0),
                  pl.BlockSpec((d, n), lambda i: (0, 0))],
        out_specs=pl.BlockSpec((tm, n), lambda i: (i, 0)),
        out_shape=jax.ShapeDtypeStruct((t, n), out_dtype),
        compiler_params=_cparams(("arbitrary",), 48),
        name="norm_mod_matmul",
    )(x2, ng, mod5, mod5, w)


def _s5_in_kernel(x_ref, g_ref, sc_ref, sh_ref, w_ref, o_ref):
    nb = x_ref.shape[0]
    wd = w_ref.shape[1]
    for b in range(nb):
        h = _norm_mod(x_ref[b], g_ref[...], sc_ref[b:b + 1, :], sh_ref[b:b + 1, :])
        o_ref[:, b * wd:(b + 1) * wd] = _bdot(h.astype(BF16), w_ref[...])


def _s5_in(x, ng, mod, w_in, *, tt):
    bsz, seq, d = x.shape
    wd = w_in.shape[1]
    mod_spec = lambda k: pl.BlockSpec((None, None, bsz, d), lambda i: (0, k, 0, 0))
    return pl.pallas_call(
        _s5_in_kernel,
        grid=(seq // tt,),
        in_specs=[pl.BlockSpec((bsz, tt, d), lambda i: (0, i, 0)),
                  pl.BlockSpec((None, 1, d), lambda i: (0, 0, 0)),
                  mod_spec(1), mod_spec(0),
                  pl.BlockSpec((d, wd), lambda i: (0, 0))],
        out_specs=pl.BlockSpec((tt, bsz * wd), lambda i: (i, 0)),
        out_shape=jax.ShapeDtypeStruct((seq, bsz * wd), F32),
        compiler_params=_cparams(("arbitrary",), 40),
        name="s5_in",
    )(x, ng, mod, mod, w_in)


def _ssm_kernel(u_ref, bbd_ref, cbd_ref, lam_ref, d_ref, y_ref, bs_ref, st_ref, *, nb, tc):
    n_slab = bbd_ref.shape[0]
    half = bbd_ref.shape[2] // 2

    @pl.when(pl.program_id(0) == 0)
    def _():
        st_ref[...] = jnp.zeros_like(st_ref)

    for j in range(n_slab):
        lanes = slice(j * LANES, (j + 1) * LANES)
        uj = u_ref[:, lanes]
        bs_ref[...] = _bdot(uj.astype(BF16), bbd_ref[j])
        lr = jnp.broadcast_to(lam_ref[j, 0:1, :], (nb, half))
        li = jnp.broadcast_to(lam_ref[j, 1:2, :], (nb, half))

        def step(t, carry):
            sr, si = carry
            r0 = pl.multiple_of(t * nb, nb)
            rows = pl.ds(r0, nb)
            nr = lr * sr - li * si + bs_ref[rows, :half]
            ni = lr * si + li * sr + bs_ref[rows, half:]
            bs_ref[rows, :half] = nr
            bs_ref[rows, half:] = ni
            return nr, ni

        sr, si = lax.fori_loop(0, tc, step, (st_ref[j, :, :half], st_ref[j, :, half:]),
                               unroll=4)
        st_ref[j, :, :half] = sr
        st_ref[j, :, half:] = si
        yj = _bdot(bs_ref[...].astype(BF16), cbd_ref[j]) + d_ref[:, lanes] * uj
        y_ref[:, lanes] = jax.nn.gelu(yj).astype(y_ref.dtype)


def _ssm(u2, bbd, cbd, lam, d_skip, *, nb, tc):
    rows, wd = u2.shape
    n_slab, _, two_p = bbd.shape
    tr = tc * nb
    return pl.pallas_call(
        functools.partial(_ssm_kernel, nb=nb, tc=tc),
        grid=(rows // tr,),
        in_specs=[pl.BlockSpec((tr, wd), lambda i: (i, 0)),
                  pl.BlockSpec(bbd.shape, lambda i: (0, 0, 0)),
                  pl.BlockSpec(cbd.shape, lambda i: (0, 0, 0)),
                  pl.BlockSpec(lam.shape, lambda i: (0, 0, 0)),
                  pl.BlockSpec((1, wd), lambda i: (0, 0))],
        out_specs=pl.BlockSpec((tr, wd), lambda i: (i, 0)),
        out_shape=jax.ShapeDtypeStruct((rows, wd), BF16),
        scratch_shapes=[pltpu.VMEM((tr, two_p), F32),
                        pltpu.VMEM((n_slab, nb, two_p), F32)],
        compiler_params=_cparams(("arbitrary",), 48),
        name="s5_scan",
    )(u2, bbd, cbd, lam, d_skip)


def _s5_discretise(lam_re, lam_im, b_re, b_im, c_re, c_im, d_skip, log_step):
    g, p, hh = b_re.shape
    gs = LANES // hh
    n_slab = g // gs
    step = jnp.exp(log_step.astype(F32))[:, None]
    lam_re, lam_im = lam_re.astype(F32), lam_im.astype(F32)
    mag = jnp.exp(lam_re * step)
    lb_re = mag * jnp.cos(lam_im * step)
    lb_im = mag * jnp.sin(lam_im * step)
    den = lam_re * lam_re + lam_im * lam_im
    coef_re = ((lb_re - 1.0) * lam_re + lb_im * lam_im) / den
    coef_im = (lb_im * lam_re - (lb_re - 1.0) * lam_im) / den
    bb_re = coef_re[..., None] * b_re - coef_im[..., None] * b_im
    bb_im = coef_re[..., None] * b_im + coef_im[..., None] * b_re
    eye = jnp.eye(gs, dtype=F32)

    def blockdiag_in(bb):
        t = bb.reshape(n_slab, gs, p, hh).transpose(0, 1, 3, 2)
        return jnp.einsum("jghp,gk->jghkp", t, eye).reshape(n_slab, gs * hh, gs * p)

    def blockdiag_out(cc):
        t = cc.reshape(n_slab, gs, hh, p).transpose(0, 1, 3, 2)
        return jnp.einsum("jgph,gk->jgpkh", t, eye).reshape(n_slab, gs * p, gs * hh)

    bbd = jnp.concatenate([blockdiag_in(bb_re), blockdiag_in(bb_im)], axis=-1).astype(BF16)
    cbd = jnp.concatenate([blockdiag_out(c_re.astype(F32)),
                           -blockdiag_out(c_im.astype(F32))], axis=1).astype(BF16)
    lam = jnp.stack([lb_re.reshape(n_slab, gs * p), lb_im.reshape(n_slab, gs * p)], axis=1)
    return bbd, cbd, lam, d_skip.astype(F32).reshape(1, g * hh)


def _s5_out_kernel(y_ref, w_ref, x_ref, g_ref, gate_ref, o_ref):
    nb = x_ref.shape[0]
    wd = w_ref.shape[0]
    d = x_ref.shape[2]
    for b in range(nb):
        z = _bdot(y_ref[:, b * wd:(b + 1) * wd], w_ref[...])
        v = z[:, :d] * jax.nn.sigmoid(z[:, d:])
        o_ref[b] = x_ref[b] + gate_ref[b:b + 1, :] * (_rms(v) * g_ref[...])


def _s5_out(y, w_out, x, ng, mod, *, tt):
    bsz, seq, d = x.shape
    wd = w_out.shape[0]
    return pl.pallas_call(
        _s5_out_kernel,
        grid=(seq // tt,),
        in_specs=[pl.BlockSpec((tt, bsz * wd), lambda i: (i, 0)),
                  pl.BlockSpec(w_out.shape, lambda i: (0, 0)),
                  pl.BlockSpec((bsz, tt, d), lambda i: (0, i, 0)),
                  pl.BlockSpec((None, 1, d), lambda i: (1, 0, 0)),
                  pl.BlockSpec((None, None, bsz, d), lambda i: (0, 2, 0, 0))],
        out_specs=pl.BlockSpec((bsz, tt, d), lambda i: (0, i, 0)),
        out_shape=jax.ShapeDtypeStruct(x.shape, F32),
        compiler_params=_cparams(("arbitrary",), 48),
        name="s5_out",
    )(y, w_out, x, ng, mod)


def _mm_post_kernel(a_ref, w_ref, x_ref, g_ref, gate_ref, o_ref):
    y = _bdot(a_ref[...], w_ref[...])
    o_ref[...] = x_ref[...] + gate_ref[...] * (_rms(y) * g_ref[...])


def _mm_post(a, w, x2, ng, mod5, layer, *, seq, tm):
    t, d = x2.shape
    k = a.shape[1]
    per_b = seq // tm
    return pl.pallas_call(
        _mm_post_kernel,
        grid=(t // tm,),
        in_specs=[pl.BlockSpec((tm, k), lambda i: (i, 0)),
                  pl.BlockSpec(w.shape, lambda i: (0, 0)),
                  pl.BlockSpec((tm, d), lambda i: (i, 0)),
                  pl.BlockSpec((None, 1, d), lambda i: (layer * 4 + 1, 0, 0)),
                  pl.BlockSpec((None, None, None, 1, d),
                               lambda i: (layer, 2, i // per_b, 0, 0))],
        out_specs=pl.BlockSpec((tm, d), lambda i: (i, 0)),
        out_shape=jax.ShapeDtypeStruct((t, d), F32),
        compiler_params=_cparams(("arbitrary",), 40),
        name="matmul_post",
    )(a, w, x2, ng, mod5)


def _ffn_kernel(x_ref, g2_ref, sc_ref, sh_ref, wg_ref, wu_ref, wd_ref, g3_ref, gate_ref,
                o_ref, h_scr, acc_scr):
    f = pl.program_id(1)

    @pl.when(f == 0)
    def _():
        h_scr[...] = _norm_mod(x_ref[...], g2_ref[...], sc_ref[...], sh_ref[...]).astype(BF16)
        acc_scr[...] = jnp.zeros_like(acc_scr)

    h = h_scr[...]
    gt = _bdot(h, wg_ref[...].astype(BF16))
    up = _bdot(h, wu_ref[...].astype(BF16))
    act = (_silu(gt) * up).astype(BF16)
    acc_scr[...] += _bdot(act, wd_ref[...].astype(BF16))

    @pl.when(f == pl.num_programs(1) - 1)
    def _():
        o_ref[...] = x_ref[...] + gate_ref[...] * (_rms(acc_scr[...]) * g3_ref[...])


def _ffn(x2, ng, mod5, layer, w_gu, w_dn, *, seq, tm, tf):
    t, d = x2.shape
    ff = w_dn.shape[0]
    nf = ff // tf
    per_b = seq // tm
    mod_spec = lambda k: pl.BlockSpec((None, None, None, 1, d),
                                      lambda i, f: (layer, k, i // per_b, 0, 0))
    return pl.pallas_call(
        _ffn_kernel,
        grid=(t // tm, nf),
        in_specs=[pl.BlockSpec((tm, d), lambda i, f: (i, 0)),
                  pl.BlockSpec((None, 1, d), lambda i, f: (layer * 4 + 2, 0, 0)),
                  mod_spec(4), mod_spec(3),
                  pl.BlockSpec((d, tf), lambda i, f: (0, f)),
                  pl.BlockSpec((d, tf), lambda i, f: (0, nf + f)),
                  pl.BlockSpec((tf, d), lambda i, f: (f, 0)),
                  pl.BlockSpec((None, 1, d), lambda i, f: (layer * 4 + 3, 0, 0)),
                  mod_spec(5)],
        out_specs=pl.BlockSpec((tm, d), lambda i, f: (i, 0)),
        out_shape=jax.ShapeDtypeStruct((t, d), F32),
        scratch_shapes=[pltpu.VMEM((tm, d), BF16), pltpu.VMEM((tm, d), F32)],
        compiler_params=_cparams(("arbitrary", "arbitrary"), 56),
        name="ffn_swiglu",
    )(x2, ng, mod5, mod5, w_gu, w_gu, w_dn, ng, mod5)


def _router_kernel(x_ref, g2_ref, sc_ref, sh_ref, wr_ref, br_ref,
                   h_ref, gsel_ref, ridx_ref, cnt_ref, run_scr, *, n_exp):
    @pl.when(pl.program_id(0) == 0)
    def _():
        run_scr[...] = jnp.zeros_like(run_scr)

    h = _norm_mod(x_ref[...], g2_ref[...], sc_ref[...], sh_ref[...])
    h_ref[...] = h
    logits = jnp.dot(h, wr_ref[...], preferred_element_type=F32,
                     precision=lax.Precision.HIGHEST) + br_ref[...]
    tm = logits.shape[0]
    lane = lax.broadcasted_iota(jnp.int32, logits.shape, 1)
    logits = jnp.where(lane < n_exp, logits, -jnp.inf)
    ex = jnp.exp(logits - jnp.max(logits, axis=-1, keepdims=True))
    probs = ex / jnp.sum(ex, axis=-1, keepdims=True)
    p1 = jnp.max(probs, axis=-1, keepdims=True)
    i1 = jnp.min(jnp.where(probs == p1, lane, LANES), axis=-1, keepdims=True)
    rest = jnp.where(lane == i1, -1.0, probs)
    p2 = jnp.max(rest, axis=-1, keepdims=True)
    i2 = jnp.min(jnp.where(rest == p2, lane, LANES), axis=-1, keepdims=True)
    den = p1 + p2
    gsel_ref[...] = jnp.where(lane == 0, p1 / den, jnp.where(lane == 1, p2 / den, 0.0))

    sel = jnp.where((lane == i1) | (lane == i2), 1.0, 0.0)
    row = lax.broadcasted_iota(jnp.int32, (tm, tm), 0)
    col = lax.broadcasted_iota(jnp.int32, (tm, tm), 1)
    earlier = jnp.where(col < row, 1.0, 0.0).astype(BF16)
    rank = _bdot(earlier, sel.astype(BF16)) + run_scr[0:1, :]
    r1 = jnp.sum(jnp.where(lane == i1, rank, 0.0), axis=-1, keepdims=True).astype(jnp.int32)
    r2 = jnp.sum(jnp.where(lane == i2, rank, 0.0), axis=-1, keepdims=True).astype(jnp.int32)
    ridx_ref[...] = jnp.where(lane == 0, i1, jnp.where(lane == 1, i2,
                              jnp.where(lane == 2, r1, jnp.where(lane == 3, r2, 0))))
    run_scr[...] = run_scr[...] + jnp.sum(sel, axis=0, keepdims=True)
    cnt_ref[...] = run_scr[...]


def _router(x2, ng, mod5, layer, w_router, b_router, *, seq, tm):
    t, d = x2.shape
    n_exp = w_router.shape[1]
    per_b = seq // tm
    wr = jnp.zeros((d, LANES), F32).at[:, :n_exp].set(w_router.astype(F32))
    br = jnp.zeros((1, LANES), F32).at[0, :n_exp].set(b_router.astype(F32))
    mod_spec = lambda k: pl.BlockSpec((None, None, None, 1, d),
                                      lambda i: (layer, k, i // per_b, 0, 0))
    row_blk = lambda w: pl.BlockSpec((tm, w), lambda i: (i, 0))
    return pl.pallas_call(
        functools.partial(_router_kernel, n_exp=n_exp),
        grid=(t // tm,),
        in_specs=[row_blk(d),
                  pl.BlockSpec((None, 1, d), lambda i: (layer * 4 + 2, 0, 0)),
                  mod_spec(4), mod_spec(3),
                  pl.BlockSpec((d, LANES), lambda i: (0, 0)),
                  pl.BlockSpec((1, LANES), lambda i: (0, 0))],
        out_specs=[row_blk(d), row_blk(LANES), row_blk(LANES),
                   pl.BlockSpec((8, LANES), lambda i: (0, 0))],
        out_shape=[jax.ShapeDtypeStruct((t, d), F32),
                   jax.ShapeDtypeStruct((t, LANES), F32),
                   jax.ShapeDtypeStruct((t, LANES), jnp.int32),
                   jax.ShapeDtypeStruct((8, LANES), F32)],
        scratch_shapes=[pltpu.VMEM((8, LANES), F32)],
        compiler_params=_cparams(("arbitrary",), 32),
        name="moe_router",
    )(x2, ng, mod5, mod5, wr, br)


def _row_copy(src_ref, src_row, dst_ref, dst_row, sem):
    return pltpu.make_async_copy(src_ref.at[pl.ds(src_row, 1)], dst_ref.at[pl.ds(dst_row, 1)], sem)


def _dispatch_kernel(pa_ref, pb_ref, h_ref, hs_in_ref, hs_ref, sem):
    del hs_in_ref
    n = h_ref.shape[0]

    def issue(r, _):
        _row_copy(h_ref, r, hs_ref, pa_ref[r], sem).start()
        _row_copy(h_ref, r, hs_ref, pb_ref[r], sem).start()
        return 0

    def drain(r, _):
        _row_copy(h_ref, r, hs_ref, pa_ref[r], sem).wait()
        _row_copy(h_ref, r, hs_ref, pb_ref[r], sem).wait()
        return 0

    lax.fori_loop(0, n, issue, 0, unroll=8)
    lax.fori_loop(0, n, drain, 0, unroll=8)


def _dispatch(h, pos_a, pos_b, hs_zero, *, tm):
    t, d = h.shape
    smem_blk = pl.BlockSpec((tm,), lambda i: (i,), memory_space=pltpu.SMEM)
    return pl.pallas_call(
        _dispatch_kernel,
        grid=(t // tm,),
        in_specs=[smem_blk, smem_blk,
                  pl.BlockSpec((tm, d), lambda i: (i, 0)),
                  pl.BlockSpec(memory_space=pl.ANY)],
        out_specs=pl.BlockSpec(memory_space=pl.ANY),
        out_shape=jax.ShapeDtypeStruct(hs_zero.shape, hs_zero.dtype),
        scratch_shapes=[pltpu.SemaphoreType.DMA],
        input_output_aliases={3: 0},
        compiler_params=_cparams(("arbitrary",), 24),
        name="moe_dispatch",
    )(pos_a, pos_b, h, hs_zero)


def _expert_kernel(te_ref, nu_ref, hs_ref, wg_ref, wu_ref, wd_ref, y_ref, h_scr, acc_scr):
    del te_ref
    f = pl.program_id(1)
    last = pl.num_programs(1) - 1
    used = pl.program_id(0) < nu_ref[0]

    @pl.when(jnp.logical_not(used) & (f == last))
    def _():
        y_ref[...] = jnp.zeros_like(y_ref)

    @pl.when(used)
    def _():
        @pl.when(f == 0)
        def _():
            h_scr[...] = hs_ref[...].astype(BF16)
            acc_scr[...] = jnp.zeros_like(acc_scr)

        h = h_scr[...]
        gt = _bdot(h, wg_ref[...].astype(BF16))
        up = _bdot(h, wu_ref[...].astype(BF16))
        act = (_silu(gt) * up).astype(BF16)
        acc_scr[...] += _bdot(act, wd_ref[...].astype(BF16))

        @pl.when(f == last)
        def _():
            y_ref[...] = acc_scr[...]


def _experts(hs, tile_expert, n_used, w_gu, w_dn, *, tm, tf):
    npad, d = hs.shape
    n_exp, ff, _ = w_dn.shape
    nf = ff // tf
    row = lambda i, f, te, nu: (jnp.minimum(i, nu[0] - 1), 0)
    fe = lambda i, f, nu: jnp.where(i < nu[0], f, nf - 1)
    return pl.pallas_call(
        _expert_kernel,
        grid_spec=pltpu.PrefetchScalarGridSpec(
            num_scalar_prefetch=2,
            grid=(npad // tm, nf),
            in_specs=[pl.BlockSpec((tm, d), row),
                      pl.BlockSpec((None, d, tf), lambda i, f, te, nu: (te[i], 0, fe(i, f, nu))),
                      pl.BlockSpec((None, d, tf),
                                   lambda i, f, te, nu: (te[i], 0, nf + fe(i, f, nu))),
                      pl.BlockSpec((None, tf, d), lambda i, f, te, nu: (te[i], fe(i, f, nu), 0))],
            out_specs=pl.BlockSpec((tm, d), lambda i, f, te, nu: (i, 0)),
            scratch_shapes=[pltpu.VMEM((tm, d), BF16), pltpu.VMEM((tm, d), F32)]),
        out_shape=jax.ShapeDtypeStruct((npad, d), F32),
        compiler_params=_cparams(("arbitrary", "arbitrary"), 56),
        name="moe_experts",
    )(tile_expert, n_used, hs, w_gu, w_gu, w_dn)


def _combine_kernel(pa_ref, pb_ref, x_ref, gsel_ref, g3_ref, gate_ref, ys_ref, o_ref,
                    ya_scr, yb_scr, sem):
    n = x_ref.shape[0]

    def issue(r, _):
        _row_copy(ys_ref, pa_ref[r], ya_scr, r, sem).start()
        _row_copy(ys_ref, pb_ref[r], yb_scr, r, sem).start()
        return 0

    def drain(r, _):
        _row_copy(ys_ref, pa_ref[r], ya_scr, r, sem).wait()
        _row_copy(ys_ref, pb_ref[r], yb_scr, r, sem).wait()
        return 0

    lax.fori_loop(0, n, issue, 0, unroll=8)
    lax.fori_loop(0, n, drain, 0, unroll=8)
    gs = gsel_ref[...]
    y = gs[:, 0:1] * ya_scr[...] + gs[:, 1:2] * yb_scr[...]
    o_ref[...] = x_ref[...] + gate_ref[...] * (_rms(y) * g3_ref[...])


def _combine(x2, gsel, pos_a, pos_b, ys, ng, mod5, layer, *, seq, tm):
    t, d = x2.shape
    per_b = seq // tm
    smem_blk = pl.BlockSpec((tm,), lambda i: (i,), memory_space=pltpu.SMEM)
    return pl.pallas_call(
        _combine_kernel,
        grid=(t // tm,),
        in_specs=[smem_blk, smem_blk,
                  pl.BlockSpec((tm, d), lambda i: (i, 0)),
                  pl.BlockSpec((tm, LANES), lambda i: (i, 0)),
                  pl.BlockSpec((None, 1, d), lambda i: (layer * 4 + 3, 0, 0)),
                  pl.BlockSpec((None, None, None, 1, d),
                               lambda i: (layer, 5, i // per_b, 0, 0)),
                  pl.BlockSpec(memory_space=pl.ANY)],
        out_specs=pl.BlockSpec((tm, d), lambda i: (i, 0)),
        out_shape=jax.ShapeDtypeStruct((t, d), F32),
        scratch_shapes=[pltpu.VMEM((tm, d), F32), pltpu.VMEM((tm, d), F32),
                        pltpu.SemaphoreType.DMA],
        compiler_params=_cparams(("arbitrary",), 40),
        name="moe_combine",
    )(pos_a, pos_b, x2, gsel, ng, mod5, ys)


def _moe(x2, ng, mod5, layer, w_router, b_router, w_gu, w_dn, *, seq, tm_tok, tm_exp, tf):
    t, d = x2.shape
    n_exp = w_router.shape[1]
    h, gsel, ridx, cnt = _router(x2, ng, mod5, layer, w_router, b_router, seq=seq, tm=tm_tok)

    counts = cnt[0, :n_exp].astype(jnp.int32)
    padded = ((counts + tm_exp - 1) // tm_exp) * tm_exp
    ends = jnp.cumsum(padded)
    starts = ends - padded
    pos_a = starts[ridx[:, 0]] + ridx[:, 2]
    pos_b = starts[ridx[:, 1]] + ridx[:, 3]
    n_tiles = (2 * t + n_exp * (tm_exp - 1)) // tm_exp
    n_used = (ends[-1] // tm_exp).reshape(1)
    tile_row = jnp.arange(n_tiles, dtype=jnp.int32) * tm_exp
    tile_expert = jnp.sum(tile_row[:, None] >= ends[None, :], axis=1).astype(jnp.int32)
    tile_expert = jnp.where(jnp.arange(n_tiles) < n_used[0], tile_expert,
                            tile_expert[n_used[0] - 1])

    hs = _dispatch(h, pos_a, pos_b, jnp.zeros((n_tiles * tm_exp, d), F32), tm=tm_tok)
    ys = _experts(hs, tile_expert, n_used, w_gu, w_dn, tm=tm_exp, tf=tf)
    return _combine(x2, gsel, pos_a, pos_b, ys, ng, mod5, layer, seq=seq, tm=tm_tok)


def _attn_kernel(q_ref, k_ref, v_ref, o_ref, acc_scr, *, tq, hd, pairs, scale):
    seq = q_ref.shape[0]
    row = lax.broadcasted_iota(jnp.int32, (tq, tq), 0)
    col = lax.broadcasted_iota(jnp.int32, (tq, tq), 1)
    strict = col < row
    after_mat = jnp.where(row > col, 1.0, 0.0).astype(BF16)
    lane = lax.broadcasted_iota(jnp.int32, (tq, LANES), 1)
    first_head = lane < hd
    heads = [(p, h) for p in range(pairs) for h in range(2)]

    def kv_block(qhs, k0, drops, masked):
        new_drops = []
        for n, (p, _) in enumerate(heads):
            lanes = slice(p * LANES, (p + 1) * LANES)
            z = lax.dot_general(qhs[n], k_ref[pl.ds(k0, tq), lanes], (((1,), (1,)), ((), ())),
                                preferred_element_type=F32)
            neg_abs = pltpu.bitcast(pltpu.bitcast(z, jnp.uint32) | jnp.uint32(0x80000000), F32)
            sp = jnp.maximum(z, 0.0) + jnp.log(1.0 + jnp.exp(neg_abs))
            if masked:
                sp = jnp.where(strict, sp, 0.0)
            drop_after = _bdot(sp.astype(BF16), after_mat) + drops[n]
            w = jnp.exp((z - sp) - drop_after)
            if masked:
                w = jnp.where(strict, w, 0.0)
            acc_scr[n] += _bdot(w.astype(BF16), v_ref[pl.ds(k0, tq), lanes])
            new_drops.append(drops[n] + jnp.sum(sp, axis=-1, keepdims=True))
        return tuple(new_drops)

    def q_block(i, _):
        q0 = pl.multiple_of(i * tq, tq)
        qhs = []
        for p, h in heads:
            q2 = q_ref[pl.ds(q0, tq), p * LANES:(p + 1) * LANES] * scale
            qhs.append(jnp.where(first_head == (h == 0), q2, jnp.zeros_like(q2)))
        acc_scr[...] = jnp.zeros_like(acc_scr)
        zero = jnp.zeros((tq, 1), F32)
        drops = kv_block(qhs, q0, (zero,) * len(heads), True)

        def live(drops):
            m = functools.reduce(jnp.minimum, drops)
            return (jnp.min(m) < DROP_DEAD).astype(jnp.int32)

        def cond(state):
            return (state[0] < i) & (state[1] > 0)

        def body(state):
            k0 = pl.multiple_of((i - 1 - state[0]) * tq, tq)
            drops = kv_block(qhs, k0, state[2:], False)
            return (state[0] + 1, live(drops)) + drops

        lax.while_loop(cond, body, (jnp.int32(0), live(drops)) + drops)
        for p in range(pairs):
            o_ref[pl.ds(q0, tq), p * LANES:(p + 1) * LANES] = jnp.where(
                first_head, acc_scr[2 * p], acc_scr[2 * p + 1]).astype(o_ref.dtype)
        return 0

    lax.fori_loop(0, seq // tq, q_block, 0)


def _attention(qkv, *, hd, tq, pairs):
    bsz, seq, three_d = qkv.shape
    d = three_d // 3
    wl = pairs * LANES
    n_step = d // wl
    blk = lambda off: pl.BlockSpec((None, seq, wl), lambda b, p: (b, 0, off + p))
    return pl.pallas_call(
        functools.partial(_attn_kernel, tq=tq, hd=hd, pairs=pairs, scale=1.0 / math.sqrt(hd)),
        grid=(bsz, n_step),
        in_specs=[blk(0), blk(n_step), blk(2 * n_step)],
        out_specs=pl.BlockSpec((None, seq, wl), lambda b, p: (b, 0, p)),
        out_shape=jax.ShapeDtypeStruct((bsz, seq, d), BF16),
        scratch_shapes=[pltpu.VMEM((2 * pairs, tq, LANES), F32)],
        compiler_params=_cparams(("arbitrary", "arbitrary"), 32),
        name="stickbreaking_attention",
    )(qkv, qkv, qkv)


def kernel(x, c, ada_w, ada_b, norm_g, s5_w_in, s5_lambda_re, s5_lambda_im, s5_b_re, s5_b_im,
           s5_c_re, s5_c_im, s5_d, s5_log_step, s5_w_out, sb_w_qkv, sb_w_out, ffn_w_gate_up,
           ffn_w_down, moe_w_router, moe_b_router, moe_w_gate_up, moe_w_down):
    bsz, seq, d = x.shape
    depth = ada_w.shape[0]
    t = bsz * seq
    hd = 64
    n_heads = d // hd

    mod = _adaln(c, ada_w, ada_b)
    mod5 = mod.reshape(depth, 6, bsz, 1, d)
    ng = norm_g.reshape(depth * 4, 1, d)

    wd = s5_w_in.shape[2]
    u = _s5_in(x, ng, mod, s5_w_in[0].astype(BF16), tt=128)
    bbd, cbd, lam, dsk = _s5_discretise(s5_lambda_re[0], s5_lambda_im[0], s5_b_re[0], s5_b_im[0],
                                        s5_c_re[0], s5_c_im[0], s5_d[0], s5_log_step[0])
    y = _ssm(u.reshape(seq * bsz, wd), bbd, cbd, lam, dsk, nb=bsz, tc=128)
    x1 = _s5_out(y.reshape(seq, bsz * wd), s5_w_out[0].astype(BF16), x, ng, mod, tt=64)
    x2 = _ffn(x1.reshape(t, d), ng, mod5, 0, ffn_w_gate_up[0], ffn_w_down[0],
              seq=seq, tm=min(1024, seq), tf=512)

    qkv = _nm_mm(x2, ng, mod5, 1, sb_w_qkv[0].astype(BF16), seq=seq, tm=512, out_dtype=BF16)
    o = _attention(qkv.reshape(bsz, seq, 3 * d), hd=hd, tq=256, pairs=4)
    x3 = _mm_post(o.reshape(t, d), sb_w_out[0].astype(BF16), x2, ng, mod5, 1, seq=seq, tm=512)
    x4 = _moe(x3, ng, mod5, 1, moe_w_router[0], moe_b_router[0], moe_w_gate_up[0],
              moe_w_down[0], seq=seq, tm_tok=min(1024, seq), tm_exp=1024, tf=512)
    return x4.reshape(bsz, seq, d)
```

```python
import functools
import math

import jax
import jax.numpy as jnp
from jax import lax
from jax.experimental import pallas as pl
from jax.experimental.pallas import tpu as pltpu

F32 = jnp.float32
BF16 = jnp.bfloat16

NORM_EPS = 1e-6
DROP_DEAD = 104.0
LANES = 128
MIB = 1024 * 1024


def _cparams(semantics, vmem_mib):
    return pltpu.CompilerParams(dimension_semantics=semantics,
                                vmem_limit_bytes=vmem_mib * MIB)


def _rms(x):
    return x * lax.rsqrt(jnp.mean(x * x, axis=-1, keepdims=True) + NORM_EPS)


def _norm_mod(x, g, scale, shift):
    return (_rms(x) * g) * (1.0 + scale) + shift


def _silu(x):
    return x * jax.nn.sigmoid(x)


def _bdot(a, b):
    return jnp.dot(a, b, preferred_element_type=F32)


def _adaln_kernel(c_ref, w_ref, b_ref, o_ref):
    ca = _silu(c_ref[...]).astype(BF16)
    o_ref[...] = _bdot(ca, w_ref[...].astype(BF16)) + b_ref[...]


def _adaln(c, ada_w, ada_b):
    depth, d, six_d = ada_w.shape
    bsz = c.shape[0]
    n6 = six_d // d
    return pl.pallas_call(
        _adaln_kernel,
        grid=(depth, n6),
        in_specs=[pl.BlockSpec((bsz, d), lambda l, k: (0, 0)),
                  pl.BlockSpec((None, d, d), lambda l, k: (l, 0, k)),
                  pl.BlockSpec((None, None, 1, d), lambda l, k: (l, k, 0, 0))],
        out_specs=pl.BlockSpec((None, None, bsz, d), lambda l, k: (l, k, 0, 0)),
        out_shape=jax.ShapeDtypeStruct((depth, n6, bsz, d), F32),
        compiler_params=_cparams(("arbitrary", "arbitrary"), 24),
        name="adaln",
    )(c, ada_w, ada_b.reshape(depth, n6, 1, d))


def _nm_mm_kernel(x_ref, g_ref, sc_ref, sh_ref, w_ref, o_ref, *, n_chunk):
    h = _norm_mod(x_ref[...], g_ref[...], sc_ref[...], sh_ref[...]).astype(BF16)
    for n0 in range(0, o_ref.shape[1], n_chunk):
        o_ref[:, n0:n0 + n_chunk] = _bdot(h, w_ref[:, n0:n0 + n_chunk]).astype(o_ref.dtype)


def _nm_mm(x2, ng, mod5, layer, w, *, seq, tm, out_dtype):
    t, d = x2.shape
    n = w.shape[1]
    assert seq % tm == 0, "row tiles must not straddle batch rows"
    per_b = seq // tm
    mod_spec = lambda k: pl.BlockSpec((None, None, None, 1, d),
                                      lambda i: (layer, k, i // per_b, 0, 0))
    return pl.pallas_call(
        functools.partial(_nm_mm_kernel, n_chunk=min(n, 1024)),
        grid=(t // tm,),
        in_specs=[pl.BlockSpec((tm, d), lambda i: (i, 0)),
                  pl.BlockSpec((None, 1, d), lambda i: (layer * 4, 0, 0)),
                  mod_spec(1), mod_spec(0),
                  pl.BlockSpec((d, n), lambda i: (0, 0))],
        out_specs=pl.BlockSpec((tm, n), lambda i: (i, 0)),
        out_shape=jax.ShapeDtypeStruct((t, n), out_dtype),
        compiler_params=_cparams(("arbitrary",), 48),
        name="norm_mod_matmul",
    )(x2, ng, mod5, mod5, w)


def _s5_kernel(x_ref, g0_ref, g1_ref, sc_ref, sh_ref, gate_ref, win_ref, bbd_ref, cbd_ref,
               lam_ref, d_ref, wout_ref, o_ref, u_scr, y_scr, bs_scr, st_scr, *, nb, tc):
    n_slab = bbd_ref.shape[0]
    half = bbd_ref.shape[2] // 2
    d = x_ref.shape[2]

    @pl.when(pl.program_id(0) == 0)
    def _():
        st_scr[...] = jnp.zeros_like(st_scr)

    for b in range(nb):
        h = _norm_mod(x_ref[b], g0_ref[...], sc_ref[b:b + 1, :], sh_ref[b:b + 1, :])
        ub = _bdot(h.astype(BF16), win_ref[...])
        for j in range(n_slab):
            u_scr[j, pl.ds(b, tc, stride=nb), :] = ub[:, j * LANES:(j + 1) * LANES]

    for j in range(n_slab):
        uj = u_scr[j]
        bs_scr[...] = _bdot(uj.astype(BF16), bbd_ref[j])
        lr = jnp.broadcast_to(lam_ref[j, 0:1, :], (nb, half))
        li = jnp.broadcast_to(lam_ref[j, 1:2, :], (nb, half))

        def step(t, carry):
            sr, si = carry
            rows = pl.ds(pl.multiple_of(t * nb, nb), nb)
            nr = lr * sr - li * si + bs_scr[rows, :half]
            ni = lr * si + li * sr + bs_scr[rows, half:]
            bs_scr[rows, :half] = nr
            bs_scr[rows, half:] = ni
            return nr, ni

        sr, si = lax.fori_loop(0, tc, step, (st_scr[j, :, :half], st_scr[j, :, half:]),
                               unroll=4)
        st_scr[j, :, :half] = sr
        st_scr[j, :, half:] = si
        yj = (_bdot(bs_scr[...].astype(BF16), cbd_ref[j])
              + d_ref[:, j * LANES:(j + 1) * LANES] * uj)
        y_scr[j] = jax.nn.gelu(yj)

    for b in range(nb):
        yb = jnp.concatenate([y_scr[j, pl.ds(b, tc, stride=nb), :] for j in range(n_slab)],
                             axis=1)
        z = _bdot(yb.astype(BF16), wout_ref[...])
        v = z[:, :d] * jax.nn.sigmoid(z[:, d:])
        o_ref[b] = x_ref[b] + gate_ref[b:b + 1, :] * (_rms(v) * g1_ref[...])


def _s5_layer(x, ng, mod, layer, w_in, bbd, cbd, lam, d_skip, w_out, *, tc):
    bsz, seq, d = x.shape
    n_slab, _, two_p = bbd.shape
    full = lambda a: pl.BlockSpec(a.shape, lambda i: (0,) * a.ndim)
    mod_spec = lambda k: pl.BlockSpec((None, None, bsz, d), lambda i: (layer, k, 0, 0))
    norm_spec = lambda k: pl.BlockSpec((None, 1, d), lambda i: (layer * 4 + k, 0, 0))
    return pl.pallas_call(
        functools.partial(_s5_kernel, nb=bsz, tc=tc),
        grid=(seq // tc,),
        in_specs=[pl.BlockSpec((bsz, tc, d), lambda i: (0, i, 0)),
                  norm_spec(0), norm_spec(1), mod_spec(1), mod_spec(0), mod_spec(2),
                  full(w_in), full(bbd), full(cbd), full(lam), full(d_skip), full(w_out)],
        out_specs=pl.BlockSpec((bsz, tc, d), lambda i: (0, i, 0)),
        out_shape=jax.ShapeDtypeStruct(x.shape, F32),
        scratch_shapes=[pltpu.VMEM((n_slab, tc * bsz, LANES), F32),
                        pltpu.VMEM((n_slab, tc * bsz, LANES), F32),
                        pltpu.VMEM((tc * bsz, two_p), F32),
                        pltpu.VMEM((n_slab, bsz, two_p), F32)],
        compiler_params=_cparams(("arbitrary",), 48),
        name="s5_mixer",
    )(x, ng, ng, mod, mod, mod, w_in, bbd, cbd, lam, d_skip, w_out)


def _s5_discretise(lam_re, lam_im, b_re, b_im, c_re, c_im, d_skip, log_step):
    g, p, hh = b_re.shape
    gs = LANES // hh
    n_slab = g // gs
    step = jnp.exp(log_step.astype(F32))[:, None]
    lam_re, lam_im = lam_re.astype(F32), lam_im.astype(F32)
    mag = jnp.exp(lam_re * step)
    lb_re = mag * jnp.cos(lam_im * step)
    lb_im = mag * jnp.sin(lam_im * step)
    den = lam_re * lam_re + lam_im * lam_im
    coef_re = ((lb_re - 1.0) * lam_re + lb_im * lam_im) / den
    coef_im = (lb_im * lam_re - (lb_re - 1.0) * lam_im) / den
    bb_re = coef_re[..., None] * b_re - coef_im[..., None] * b_im
    bb_im = coef_re[..., None] * b_im + coef_im[..., None] * b_re
    eye = jnp.eye(gs, dtype=F32)

    def blockdiag_in(bb):
        t = bb.reshape(n_slab, gs, p, hh).transpose(0, 1, 3, 2)
        return jnp.einsum("jghp,gk->jghkp", t, eye).reshape(n_slab, gs * hh, gs * p)

    def blockdiag_out(cc):
        t = cc.reshape(n_slab, gs, hh, p).transpose(0, 1, 3, 2)
        return jnp.einsum("jgph,gk->jgpkh", t, eye).reshape(n_slab, gs * p, gs * hh)

    bbd = jnp.concatenate([blockdiag_in(bb_re), blockdiag_in(bb_im)], axis=-1).astype(BF16)
    cbd = jnp.concatenate([blockdiag_out(c_re.astype(F32)),
                           -blockdiag_out(c_im.astype(F32))], axis=1).astype(BF16)
    lam = jnp.stack([lb_re.reshape(n_slab, gs * p), lb_im.reshape(n_slab, gs * p)], axis=1)
    return bbd, cbd, lam, d_skip.astype(F32).reshape(1, g * hh)


def _mm_post_kernel(a_ref, w_ref, x_ref, g_ref, gate_ref, o_ref):
    y = _bdot(a_ref[...], w_ref[...])
    o_ref[...] = x_ref[...] + gate_ref[...] * (_rms(y) * g_ref[...])


def _mm_post(a, w, x2, ng, mod5, layer, *, seq, tm):
    t, d = x2.shape
    k = a.shape[1]
    assert seq % tm == 0, "row tiles must not straddle batch rows"
    per_b = seq // tm
    return pl.pallas_call(
        _mm_post_kernel,
        grid=(t // tm,),
        in_specs=[pl.BlockSpec((tm, k), lambda i: (i, 0)),
                  pl.BlockSpec(w.shape, lambda i: (0, 0)),
                  pl.BlockSpec((tm, d), lambda i: (i, 0)),
                  pl.BlockSpec((None, 1, d), lambda i: (layer * 4 + 1, 0, 0)),
                  pl.BlockSpec((None, None, None, 1, d),
                               lambda i: (layer, 2, i // per_b, 0, 0))],
        out_specs=pl.BlockSpec((tm, d), lambda i: (i, 0)),
        out_shape=jax.ShapeDtypeStruct((t, d), F32),
        compiler_params=_cparams(("arbitrary",), 40),
        name="matmul_post",
    )(a, w, x2, ng, mod5)


def _ffn_kernel(x_ref, g2_ref, sc_ref, sh_ref, wg_ref, wu_ref, wd_ref, g3_ref, gate_ref,
                o_ref, h_scr, acc_scr):
    f = pl.program_id(1)

    @pl.when(f == 0)
    def _():
        h_scr[...] = _norm_mod(x_ref[...], g2_ref[...], sc_ref[...], sh_ref[...]).astype(BF16)
        acc_scr[...] = jnp.zeros_like(acc_scr)

    h = h_scr[...]
    gt = _bdot(h, wg_ref[...])
    up = _bdot(h, wu_ref[...])
    act = (_silu(gt) * up).astype(BF16)
    acc_scr[...] += _bdot(act, wd_ref[...])

    @pl.when(f == pl.num_programs(1) - 1)
    def _():
        o_ref[...] = x_ref[...] + gate_ref[...] * (_rms(acc_scr[...]) * g3_ref[...])


def _block_gate_up(w_gu, tf):
    *lead, d, two_f = w_gu.shape
    w = w_gu.astype(BF16).reshape(*lead, d, two_f // tf, tf)
    return jnp.swapaxes(w, -3, -2)


def _ffn(x2, ng, mod5, layer, w_gu, w_dn, *, seq, tm, tf):
    t, d = x2.shape
    ff = w_dn.shape[0]
    nf = ff // tf
    assert seq % tm == 0, "row tiles must not straddle batch rows"
    per_b = seq // tm
    mod_spec = lambda k: pl.BlockSpec((None, None, None, 1, d),
                                      lambda i, f: (layer, k, i // per_b, 0, 0))
    return pl.pallas_call(
        _ffn_kernel,
        grid=(t // tm, nf),
        in_specs=[pl.BlockSpec((tm, d), lambda i, f: (i, 0)),
                  pl.BlockSpec((None, 1, d), lambda i, f: (layer * 4 + 2, 0, 0)),
                  mod_spec(4), mod_spec(3),
                  pl.BlockSpec((None, d, tf), lambda i, f: (f, 0, 0)),
                  pl.BlockSpec((None, d, tf), lambda i, f: (nf + f, 0, 0)),
                  pl.BlockSpec((tf, d), lambda i, f: (f, 0)),
                  pl.BlockSpec((None, 1, d), lambda i, f: (layer * 4 + 3, 0, 0)),
                  mod_spec(5)],
        out_specs=pl.BlockSpec((tm, d), lambda i, f: (i, 0)),
        out_shape=jax.ShapeDtypeStruct((t, d), F32),
        scratch_shapes=[pltpu.VMEM((tm, d), BF16), pltpu.VMEM((tm, d), F32)],
        compiler_params=_cparams(("arbitrary", "arbitrary"), 56),
        name="ffn_swiglu",
    )(x2, ng, mod5, mod5, w_gu, w_gu, w_dn, ng, mod5)


def _router_kernel(x_ref, g2_ref, sc_ref, sh_ref, wr_ref, br_ref,
                   h_ref, gsel_ref, ridx_ref, cnt_ref, run_scr, *, n_exp):
    @pl.when(pl.program_id(0) == 0)
    def _():
        run_scr[...] = jnp.zeros_like(run_scr)

    h = _norm_mod(x_ref[...], g2_ref[...], sc_ref[...], sh_ref[...])
    h_ref[...] = h
    logits = jnp.dot(h, wr_ref[...], preferred_element_type=F32,
                     precision=lax.Precision.HIGHEST) + br_ref[...]
    tm = logits.shape[0]
    lane = lax.broadcasted_iota(jnp.int32, logits.shape, 1)
    logits = jnp.where(lane < n_exp, logits, -jnp.inf)
    ex = jnp.exp(logits - jnp.max(logits, axis=-1, keepdims=True))
    probs = ex / jnp.sum(ex, axis=-1, keepdims=True)
    p1 = jnp.max(probs, axis=-1, keepdims=True)
    i1 = jnp.min(jnp.where(probs == p1, lane, LANES), axis=-1, keepdims=True)
    rest = jnp.where(lane == i1, -1.0, probs)
    p2 = jnp.max(rest, axis=-1, keepdims=True)
    i2 = jnp.min(jnp.where(rest == p2, lane, LANES), axis=-1, keepdims=True)
    den = p1 + p2
    gsel_ref[...] = jnp.where(lane == 0, p1 / den, jnp.where(lane == 1, p2 / den, 0.0))

    sel = jnp.where((lane == i1) | (lane == i2), 1.0, 0.0)
    row = lax.broadcasted_iota(jnp.int32, (tm, tm), 0)
    col = lax.broadcasted_iota(jnp.int32, (tm, tm), 1)
    earlier = jnp.where(col < row, 1.0, 0.0).astype(BF16)
    rank = _bdot(earlier, sel.astype(BF16)) + run_scr[0:1, :]
    r1 = jnp.sum(jnp.where(lane == i1, rank, 0.0), axis=-1, keepdims=True).astype(jnp.int32)
    r2 = jnp.sum(jnp.where(lane == i2, rank, 0.0), axis=-1, keepdims=True).astype(jnp.int32)
    ridx_ref[...] = jnp.where(lane == 0, i1, jnp.where(lane == 1, i2,
                              jnp.where(lane == 2, r1, jnp.where(lane == 3, r2, 0))))
    run_scr[...] = run_scr[...] + jnp.sum(sel, axis=0, keepdims=True)
    cnt_ref[...] = run_scr[...]


def _router(x2, ng, mod5, layer, w_router, b_router, *, seq, tm):
    t, d = x2.shape
    n_exp = w_router.shape[1]
    assert seq % tm == 0, "row tiles must not straddle batch rows"
    per_b = seq // tm
    wr = jnp.zeros((d, LANES), F32).at[:, :n_exp].set(w_router.astype(F32))
    br = jnp.zeros((1, LANES), F32).at[0, :n_exp].set(b_router.astype(F32))
    mod_spec = lambda k: pl.BlockSpec((None, None, None, 1, d),
                                      lambda i: (layer, k, i // per_b, 0, 0))
    row_blk = lambda w: pl.BlockSpec((tm, w), lambda i: (i, 0))
    return pl.pallas_call(
        functools.partial(_router_kernel, n_exp=n_exp),
        grid=(t // tm,),
        in_specs=[row_blk(d),
                  pl.BlockSpec((None, 1, d), lambda i: (layer * 4 + 2, 0, 0)),
                  mod_spec(4), mod_spec(3),
                  pl.BlockSpec((d, LANES), lambda i: (0, 0)),
                  pl.BlockSpec((1, LANES), lambda i: (0, 0))],
        out_specs=[row_blk(d), row_blk(LANES), row_blk(LANES),
                   pl.BlockSpec((8, LANES), lambda i: (0, 0))],
        out_shape=[jax.ShapeDtypeStruct((t, d), F32),
                   jax.ShapeDtypeStruct((t, LANES), F32),
                   jax.ShapeDtypeStruct((t, LANES), jnp.int32),
                   jax.ShapeDtypeStruct((8, LANES), F32)],
        scratch_shapes=[pltpu.VMEM((8, LANES), F32)],
        compiler_params=_cparams(("arbitrary",), 32),
        name="moe_router",
    )(x2, ng, mod5, mod5, wr, br)


def _row_copy(src_ref, src_row, dst_ref, dst_row, sem):
    return pltpu.make_async_copy(src_ref.at[pl.ds(src_row, 1)], dst_ref.at[pl.ds(dst_row, 1)], sem)


def _dispatch_kernel(pa_ref, pb_ref, h_ref, hs_in_ref, hs_ref, sem):
    del hs_in_ref
    n = h_ref.shape[0]

    def issue(r, _):
        _row_copy(h_ref, r, hs_ref, pa_ref[r], sem).start()
        _row_copy(h_ref, r, hs_ref, pb_ref[r], sem).start()
        return 0

    def drain(r, _):
        _row_copy(h_ref, r, hs_ref, pa_ref[r], sem).wait()
        _row_copy(h_ref, r, hs_ref, pb_ref[r], sem).wait()
        return 0

    lax.fori_loop(0, n, issue, 0, unroll=8)
    lax.fori_loop(0, n, drain, 0, unroll=8)


def _dispatch(h, pos_a, pos_b, hs_zero, *, tm):
    t, d = h.shape
    smem_blk = pl.BlockSpec((tm,), lambda i: (i,), memory_space=pltpu.SMEM)
    return pl.pallas_call(
        _dispatch_kernel,
        grid=(t // tm,),
        in_specs=[smem_blk, smem_blk,
                  pl.BlockSpec((tm, d), lambda i: (i, 0)),
                  pl.BlockSpec(memory_space=pl.ANY)],
        out_specs=pl.BlockSpec(memory_space=pl.ANY),
        out_shape=jax.ShapeDtypeStruct(hs_zero.shape, hs_zero.dtype),
        scratch_shapes=[pltpu.SemaphoreType.DMA],
        input_output_aliases={3: 0},
        compiler_params=_cparams(("arbitrary",), 24),
        name="moe_dispatch",
    )(pos_a, pos_b, h, hs_zero)


def _expert_kernel(te_ref, nu_ref, hs_ref, wg_ref, wu_ref, wd_ref, y_ref, h_scr, acc_scr):
    del te_ref
    f = pl.program_id(1)
    last = pl.num_programs(1) - 1
    used = pl.program_id(0) < nu_ref[0]

    @pl.when(jnp.logical_not(used) & (f == last))
    def _():
        y_ref[...] = jnp.zeros_like(y_ref)

    @pl.when(used)
    def _():
        @pl.when(f == 0)
        def _():
            h_scr[...] = hs_ref[...].astype(BF16)
            acc_scr[...] = jnp.zeros_like(acc_scr)

        h = h_scr[...]
        gt = _bdot(h, wg_ref[...])
        up = _bdot(h, wu_ref[...])
        act = (_silu(gt) * up).astype(BF16)
        acc_scr[...] += _bdot(act, wd_ref[...])

        @pl.when(f == last)
        def _():
            y_ref[...] = acc_scr[...]


def _experts(hs, tile_expert, n_used, w_gu, w_dn, *, tm, tf):
    npad, d = hs.shape
    n_exp, ff, _ = w_dn.shape
    nf = ff // tf
    row = lambda i, f, te, nu: (jnp.minimum(i, nu[0] - 1), 0)
    fe = lambda i, f, nu: jnp.where(i < nu[0], f, nf - 1)
    return pl.pallas_call(
        _expert_kernel,
        grid_spec=pltpu.PrefetchScalarGridSpec(
            num_scalar_prefetch=2,
            grid=(npad // tm, nf),
            in_specs=[pl.BlockSpec((tm, d), row),
                      pl.BlockSpec((None, None, d, tf),
                                   lambda i, f, te, nu: (te[i], fe(i, f, nu), 0, 0)),
                      pl.BlockSpec((None, None, d, tf),
                                   lambda i, f, te, nu: (te[i], nf + fe(i, f, nu), 0, 0)),
                      pl.BlockSpec((None, tf, d), lambda i, f, te, nu: (te[i], fe(i, f, nu), 0))],
            out_specs=pl.BlockSpec((tm, d), lambda i, f, te, nu: (i, 0)),
            scratch_shapes=[pltpu.VMEM((tm, d), BF16), pltpu.VMEM((tm, d), F32)]),
        out_shape=jax.ShapeDtypeStruct((npad, d), F32),
        compiler_params=_cparams(("arbitrary", "arbitrary"), 56),
        name="moe_experts",
    )(tile_expert, n_used, hs, w_gu, w_gu, w_dn)


def _combine_kernel(pa_ref, pb_ref, x_ref, gsel_ref, g3_ref, gate_ref, ys_ref, o_ref,
                    ya_scr, yb_scr, sem):
    n = x_ref.shape[0]

    def issue(r, _):
        _row_copy(ys_ref, pa_ref[r], ya_scr, r, sem).start()
        _row_copy(ys_ref, pb_ref[r], yb_scr, r, sem).start()
        return 0

    def drain(r, _):
        _row_copy(ys_ref, pa_ref[r], ya_scr, r, sem).wait()
        _row_copy(ys_ref, pb_ref[r], yb_scr, r, sem).wait()
        return 0

    lax.fori_loop(0, n, issue, 0, unroll=8)
    lax.fori_loop(0, n, drain, 0, unroll=8)
    gs = gsel_ref[...]
    y = gs[:, 0:1] * ya_scr[...] + gs[:, 1:2] * yb_scr[...]
    o_ref[...] = x_ref[...] + gate_ref[...] * (_rms(y) * g3_ref[...])


def _combine(x2, gsel, pos_a, pos_b, ys, ng, mod5, layer, *, seq, tm):
    t, d = x2.shape
    assert seq % tm == 0, "row tiles must not straddle batch rows"
    per_b = seq // tm
    smem_blk = pl.BlockSpec((tm,), lambda i: (i,), memory_space=pltpu.SMEM)
    return pl.pallas_call(
        _combine_kernel,
        grid=(t // tm,),
        in_specs=[smem_blk, smem_blk,
                  pl.BlockSpec((tm, d), lambda i: (i, 0)),
                  pl.BlockSpec((tm, LANES), lambda i: (i, 0)),
                  pl.BlockSpec((None, 1, d), lambda i: (layer * 4 + 3, 0, 0)),
                  pl.BlockSpec((None, None, None, 1, d),
                               lambda i: (layer, 5, i // per_b, 0, 0)),
                  pl.BlockSpec(memory_space=pl.ANY)],
        out_specs=pl.BlockSpec((tm, d), lambda i: (i, 0)),
        out_shape=jax.ShapeDtypeStruct((t, d), F32),
        scratch_shapes=[pltpu.VMEM((tm, d), F32), pltpu.VMEM((tm, d), F32),
                        pltpu.SemaphoreType.DMA],
        compiler_params=_cparams(("arbitrary",), 40),
        name="moe_combine",
    )(pos_a, pos_b, x2, gsel, ng, mod5, ys)


def _moe(x2, ng, mod5, layer, w_router, b_router, w_gu, w_dn, *, seq, tm_tok, tm_exp, tf):
    t, d = x2.shape
    n_exp = w_router.shape[1]
    h, gsel, ridx, cnt = _router(x2, ng, mod5, layer, w_router, b_router, seq=seq, tm=tm_tok)

    counts = cnt[0, :n_exp].astype(jnp.int32)
    padded = ((counts + tm_exp - 1) // tm_exp) * tm_exp
    ends = jnp.cumsum(padded)
    starts = ends - padded
    pos_a = starts[ridx[:, 0]] + ridx[:, 2]
    pos_b = starts[ridx[:, 1]] + ridx[:, 3]
    n_tiles = (2 * t + n_exp * (tm_exp - 1)) // tm_exp
    n_used = (ends[-1] // tm_exp).reshape(1)
    tile_row = jnp.arange(n_tiles, dtype=jnp.int32) * tm_exp
    tile_expert = jnp.sum(tile_row[:, None] >= ends[None, :], axis=1).astype(jnp.int32)
    tile_expert = jnp.where(jnp.arange(n_tiles) < n_used[0], tile_expert,
                            tile_expert[n_used[0] - 1])

    hs = _dispatch(h, pos_a, pos_b, jnp.zeros((n_tiles * tm_exp, d), F32), tm=tm_tok)
    ys = _experts(hs, tile_expert, n_used, w_gu, w_dn, tm=tm_exp, tf=tf)
    return _combine(x2, gsel, pos_a, pos_b, ys, ng, mod5, layer, seq=seq, tm=tm_tok)


def _attn_kernel(q_ref, k_ref, v_ref, o_ref, acc_scr, *, tq, hd, pairs, scale):
    seq = q_ref.shape[0]
    row = lax.broadcasted_iota(jnp.int32, (tq, tq), 0)
    col = lax.broadcasted_iota(jnp.int32, (tq, tq), 1)
    strict = col < row
    after_mat = jnp.where(row > col, 1.0, 0.0).astype(BF16)
    lane = lax.broadcasted_iota(jnp.int32, (tq, LANES), 1)
    first_head = lane < hd
    heads = [(p, h) for p in range(pairs) for h in range(2)]

    def kv_block(qhs, k0, drops, masked):
        new_drops = []
        for n, (p, _) in enumerate(heads):
            lanes = slice(p * LANES, (p + 1) * LANES)
            z = lax.dot_general(qhs[n], k_ref[pl.ds(k0, tq), lanes], (((1,), (1,)), ((), ())),
                                preferred_element_type=F32)
            sp = jnp.maximum(z, 0.0) + jnp.log(1.0 + jnp.exp(-jnp.abs(z)))
            if masked:
                sp = jnp.where(strict, sp, 0.0)
            drop_after = _bdot(sp.astype(BF16), after_mat) + drops[n]
            w = jnp.exp((z - sp) - drop_after)
            if masked:
                w = jnp.where(strict, w, 0.0)
            acc_scr[n] += _bdot(w.astype(BF16), v_ref[pl.ds(k0, tq), lanes])
            new_drops.append(drops[n] + jnp.sum(sp, axis=-1, keepdims=True))
        return tuple(new_drops)

    def q_block(i, _):
        q0 = pl.multiple_of(i * tq, tq)
        qhs = []
        for p, h in heads:
            q2 = q_ref[pl.ds(q0, tq), p * LANES:(p + 1) * LANES] * scale
            qhs.append(jnp.where(first_head == (h == 0), q2, jnp.zeros_like(q2)))
        acc_scr[...] = jnp.zeros_like(acc_scr)
        zero = jnp.zeros((tq, 1), F32)
        drops = kv_block(qhs, q0, (zero,) * len(heads), True)

        def live(drops):
            m = functools.reduce(jnp.minimum, drops)
            return (jnp.min(m) < DROP_DEAD).astype(jnp.int32)

        def cond(state):
            return (state[0] < i) & (state[1] > 0)

        def body(state):
            k0 = pl.multiple_of((i - 1 - state[0]) * tq, tq)
            drops = kv_block(qhs, k0, state[2:], False)
            return (state[0] + 1, live(drops)) + drops

        lax.while_loop(cond, body, (jnp.int32(0), live(drops)) + drops)
        for p in range(pairs):
            o_ref[pl.ds(q0, tq), p * LANES:(p + 1) * LANES] = jnp.where(
                first_head, acc_scr[2 * p], acc_scr[2 * p + 1]).astype(o_ref.dtype)
        return 0

    lax.fori_loop(0, seq // tq, q_block, 0)


def _attention(qkv, *, hd, tq, pairs):
    bsz, seq, three_d = qkv.shape
    d = three_d // 3
    wl = pairs * LANES
    n_step = d // wl
    blk = lambda off: pl.BlockSpec((None, seq, wl), lambda b, p: (b, 0, off + p))
    return pl.pallas_call(
        functools.partial(_attn_kernel, tq=tq, hd=hd, pairs=pairs, scale=1.0 / math.sqrt(hd)),
        grid=(bsz, n_step),
        in_specs=[blk(0), blk(n_step), blk(2 * n_step)],
        out_specs=pl.BlockSpec((None, seq, wl), lambda b, p: (b, 0, p)),
        out_shape=jax.ShapeDtypeStruct((bsz, seq, d), BF16),
        scratch_shapes=[pltpu.VMEM((2 * pairs, tq, LANES), F32)],
        compiler_params=_cparams(("arbitrary", "arbitrary"), 32),
        name="stickbreaking_attention",
    )(qkv, qkv, qkv)


def kernel(x, c, ada_w, ada_b, norm_g, s5_w_in, s5_lambda_re, s5_lambda_im, s5_b_re, s5_b_im,
           s5_c_re, s5_c_im, s5_d, s5_log_step, s5_w_out, sb_w_qkv, sb_w_out, ffn_w_gate_up,
           ffn_w_down, moe_w_router, moe_b_router, moe_w_gate_up, moe_w_down):
    bsz, seq, d = x.shape
    depth = ada_w.shape[0]
    t = bsz * seq
    hd = 64
    tm = min(1024, seq)
    tf = 512

    mod = _adaln(c, ada_w, ada_b)
    mod5 = mod.reshape(depth, 6, bsz, 1, d)
    ng = norm_g.reshape(depth * 4, 1, d)

    bbd, cbd, lam, dsk = _s5_discretise(s5_lambda_re[0], s5_lambda_im[0], s5_b_re[0], s5_b_im[0],
                                        s5_c_re[0], s5_c_im[0], s5_d[0], s5_log_step[0])
    x1 = _s5_layer(x, ng, mod, 0, s5_w_in[0].astype(BF16), bbd, cbd, lam, dsk,
                   s5_w_out[0].astype(BF16), tc=64)
    x2 = _ffn(x1.reshape(t, d), ng, mod5, 0, _block_gate_up(ffn_w_gate_up[0], tf),
              ffn_w_down[0].astype(BF16), seq=seq, tm=tm, tf=tf)

    qkv = _nm_mm(x2, ng, mod5, 1, sb_w_qkv[0].astype(BF16), seq=seq, tm=tm, out_dtype=BF16)
    o = _attention(qkv.reshape(bsz, seq, 3 * d), hd=hd, tq=256, pairs=4)
    x3 = _mm_post(o.reshape(t, d), sb_w_out[0].astype(BF16), x2, ng, mod5, 1, seq=seq, tm=tm)
    x4 = _moe(x3, ng, mod5, 1, moe_w_router[0], moe_b_router[0],
              _block_gate_up(moe_w_gate_up[0], tf), moe_w_down[0].astype(BF16),
              seq=seq, tm_tok=tm, tm_exp=1024, tf=tf)
    return x4.reshape(bsz, seq, d)
```

```python
import functools
import math

import jax
import jax.numpy as jnp
from jax import lax
from jax.experimental import pallas as pl
from jax.experimental.pallas import tpu as pltpu

F32 = jnp.float32
BF16 = jnp.bfloat16

NORM_EPS = 1e-6
DROP_DEAD = 104.0
LANES = 128
MIB = 1024 * 1024


def _cparams(semantics, vmem_mib):
    return pltpu.CompilerParams(dimension_semantics=semantics,
                                vmem_limit_bytes=vmem_mib * MIB)


def _rms(x):
    return x * lax.rsqrt(jnp.mean(x * x, axis=-1, keepdims=True) + NORM_EPS)


def _norm_mod(x, g, scale, shift):
    return (_rms(x) * g) * (1.0 + scale) + shift


def _silu(x):
    return x * jax.nn.sigmoid(x)


def _bdot(a, b):
    return jnp.dot(a, b, preferred_element_type=F32)


def _adaln_kernel(c_ref, w_ref, b_ref, o_ref):
    ca = _silu(c_ref[...]).astype(BF16)
    o_ref[...] = _bdot(ca, w_ref[...].astype(BF16)) + b_ref[...]


def _adaln(c, ada_w, ada_b):
    depth, d, six_d = ada_w.shape
    bsz = c.shape[0]
    n6 = six_d // d
    return pl.pallas_call(
        _adaln_kernel,
        grid=(depth, n6),
        in_specs=[pl.BlockSpec((bsz, d), lambda l, k: (0, 0)),
                  pl.BlockSpec((None, d, d), lambda l, k: (l, 0, k)),
                  pl.BlockSpec((None, None, 1, d), lambda l, k: (l, k, 0, 0))],
        out_specs=pl.BlockSpec((None, None, bsz, d), lambda l, k: (l, k, 0, 0)),
        out_shape=jax.ShapeDtypeStruct((depth, n6, bsz, d), F32),
        compiler_params=_cparams(("arbitrary", "arbitrary"), 24),
        name="adaln",
    )(c, ada_w, ada_b.reshape(depth, n6, 1, d))


def _nm_mm_kernel(x_ref, g_ref, sc_ref, sh_ref, w_ref, o_ref, *, n_chunk):
    h = _norm_mod(x_ref[...], g_ref[...], sc_ref[...], sh_ref[...]).astype(BF16)
    for n0 in range(0, o_ref.shape[1], n_chunk):
        o_ref[:, n0:n0 + n_chunk] = _bdot(h, w_ref[:, n0:n0 + n_chunk]).astype(o_ref.dtype)


def _nm_mm(x2, ng, mod5, layer, w, *, seq, tm, out_dtype):
    t, d = x2.shape
    n = w.shape[1]
    assert seq % tm == 0, "row tiles must not straddle batch rows"
    per_b = seq // tm
    mod_spec = lambda k: pl.BlockSpec((None, None, None, 1, d),
                                      lambda i: (layer, k, i // per_b, 0, 0))
    return pl.pallas_call(
        functools.partial(_nm_mm_kernel, n_chunk=min(n, 1024)),
        grid=(t // tm,),
        in_specs=[pl.BlockSpec((tm, d), lambda i: (i, 0)),
                  pl.BlockSpec((None, 1, d), lambda i: (layer * 4, 0, 0)),
                  mod_spec(1), mod_spec(0),
                  pl.BlockSpec((d, n), lambda i: (0, 0))],
        out_specs=pl.BlockSpec((tm, n), lambda i: (i, 0)),
        out_shape=jax.ShapeDtypeStruct((t, n), out_dtype),
        compiler_params=_cparams(("arbitrary",), 48),
        name="norm_mod_matmul",
    )(x2, ng, mod5, mod5, w)


def _s5_kernel(x_ref, g0_ref, g1_ref, sc_ref, sh_ref, gate_ref, win_ref, bbd_ref, cbd_ref,
               lam_ref, d_ref, wout_ref, o_ref, u_scr, y_scr, bs_scr, st_scr, *, nb, tc):
    n_slab = bbd_ref.shape[0]
    half = bbd_ref.shape[2] // 2
    d = x_ref.shape[2]

    @pl.when(pl.program_id(0) == 0)
    def _():
        st_scr[...] = jnp.zeros_like(st_scr)

    for b in range(nb):
        h = _norm_mod(x_ref[b], g0_ref[...], sc_ref[b:b + 1, :], sh_ref[b:b + 1, :])
        ub = _bdot(h.astype(BF16), win_ref[...])
        for j in range(n_slab):
            u_scr[j, pl.ds(b, tc, stride=nb), :] = ub[:, j * LANES:(j + 1) * LANES]

    for j in range(n_slab):
        uj = u_scr[j]
        bs_scr[...] = _bdot(uj.astype(BF16), bbd_ref[j])
        lr = jnp.broadcast_to(lam_ref[j, 0:1, :], (nb, half))
        li = jnp.broadcast_to(lam_ref[j, 1:2, :], (nb, half))

        def step(t, carry):
            sr, si = carry
            rows = pl.ds(pl.multiple_of(t * nb, nb), nb)
            nr = lr * sr - li * si + bs_scr[rows, :half]
            ni = lr * si + li * sr + bs_scr[rows, half:]
            bs_scr[rows, :half] = nr
            bs_scr[rows, half:] = ni
            return nr, ni

        sr, si = lax.fori_loop(0, tc, step, (st_scr[j, :, :half], st_scr[j, :, half:]),
                               unroll=4)
        st_scr[j, :, :half] = sr
        st_scr[j, :, half:] = si
        yj = (_bdot(bs_scr[...].astype(BF16), cbd_ref[j])
              + d_ref[:, j * LANES:(j + 1) * LANES] * uj)
        y_scr[j] = jax.nn.gelu(yj)

    for b in range(nb):
        yb = jnp.concatenate([y_scr[j, pl.ds(b, tc, stride=nb), :] for j in range(n_slab)],
                             axis=1)
        z = _bdot(yb.astype(BF16), wout_ref[...])
        v = z[:, :d] * jax.nn.sigmoid(z[:, d:])
        o_ref[b] = x_ref[b] + gate_ref[b:b + 1, :] * (_rms(v) * g1_ref[...])


def _s5_layer(x, ng, mod, layer, w_in, bbd, cbd, lam, d_skip, w_out, *, tc):
    bsz, seq, d = x.shape
    n_slab, _, two_p = bbd.shape
    full = lambda a: pl.BlockSpec(a.shape, lambda i: (0,) * a.ndim)
    mod_spec = lambda k: pl.BlockSpec((None, None, bsz, d), lambda i: (layer, k, 0, 0))
    norm_spec = lambda k: pl.BlockSpec((None, 1, d), lambda i: (layer * 4 + k, 0, 0))
    return pl.pallas_call(
        functools.partial(_s5_kernel, nb=bsz, tc=tc),
        grid=(seq // tc,),
        in_specs=[pl.BlockSpec((bsz, tc, d), lambda i: (0, i, 0)),
                  norm_spec(0), norm_spec(1), mod_spec(1), mod_spec(0), mod_spec(2),
                  full(w_in), full(bbd), full(cbd), full(lam), full(d_skip), full(w_out)],
        out_specs=pl.BlockSpec((bsz, tc, d), lambda i: (0, i, 0)),
        out_shape=jax.ShapeDtypeStruct(x.shape, F32),
        scratch_shapes=[pltpu.VMEM((n_slab, tc * bsz, LANES), F32),
                        pltpu.VMEM((n_slab, tc * bsz, LANES), F32),
                        pltpu.VMEM((tc * bsz, two_p), F32),
                        pltpu.VMEM((n_slab, bsz, two_p), F32)],
        compiler_params=_cparams(("arbitrary",), 48),
        name="s5_mixer",
    )(x, ng, ng, mod, mod, mod, w_in, bbd, cbd, lam, d_skip, w_out)


def _s5_discretise(lam_re, lam_im, b_re, b_im, c_re, c_im, d_skip, log_step):
    g, p, hh = b_re.shape
    gs = LANES // hh
    n_slab = g // gs
    step = jnp.exp(log_step.astype(F32))[:, None]
    lam_re, lam_im = lam_re.astype(F32), lam_im.astype(F32)
    mag = jnp.exp(lam_re * step)
    lb_re = mag * jnp.cos(lam_im * step)
    lb_im = mag * jnp.sin(lam_im * step)
    den = lam_re * lam_re + lam_im * lam_im
    coef_re = ((lb_re - 1.0) * lam_re + lb_im * lam_im) / den
    coef_im = (lb_im * lam_re - (lb_re - 1.0) * lam_im) / den
    bb_re = coef_re[..., None] * b_re - coef_im[..., None] * b_im
    bb_im = coef_re[..., None] * b_im + coef_im[..., None] * b_re
    eye = jnp.eye(gs, dtype=F32)

    def blockdiag_in(bb):
        t = bb.reshape(n_slab, gs, p, hh).transpose(0, 1, 3, 2)
        return jnp.einsum("jghp,gk->jghkp", t, eye).reshape(n_slab, gs * hh, gs * p)

    def blockdiag_out(cc):
        t = cc.reshape(n_slab, gs, hh, p).transpose(0, 1, 3, 2)
        return jnp.einsum("jgph,gk->jgpkh", t, eye).reshape(n_slab, gs * p, gs * hh)

    bbd = jnp.concatenate([blockdiag_in(bb_re), blockdiag_in(bb_im)], axis=-1).astype(BF16)
    cbd = jnp.concatenate([blockdiag_out(c_re.astype(F32)),
                           -blockdiag_out(c_im.astype(F32))], axis=1).astype(BF16)
    lam = jnp.stack([lb_re.reshape(n_slab, gs * p), lb_im.reshape(n_slab, gs * p)], axis=1)
    return bbd, cbd, lam, d_skip.astype(F32).reshape(1, g * hh)


def _mm_post_kernel(a_ref, w_ref, x_ref, g_ref, gate_ref, o_ref):
    y = _bdot(a_ref[...], w_ref[...])
    o_ref[...] = x_ref[...] + gate_ref[...] * (_rms(y) * g_ref[...])


def _mm_post(a, w, x2, ng, mod5, layer, *, seq, tm):
    t, d = x2.shape
    k = a.shape[1]
    assert seq % tm == 0, "row tiles must not straddle batch rows"
    per_b = seq // tm
    return pl.pallas_call(
        _mm_post_kernel,
        grid=(t // tm,),
        in_specs=[pl.BlockSpec((tm, k), lambda i: (i, 0)),
                  pl.BlockSpec(w.shape, lambda i: (0, 0)),
                  pl.BlockSpec((tm, d), lambda i: (i, 0)),
                  pl.BlockSpec((None, 1, d), lambda i: (layer * 4 + 1, 0, 0)),
                  pl.BlockSpec((None, None, None, 1, d),
                               lambda i: (layer, 2, i // per_b, 0, 0))],
        out_specs=pl.BlockSpec((tm, d), lambda i: (i, 0)),
        out_shape=jax.ShapeDtypeStruct((t, d), F32),
        compiler_params=_cparams(("arbitrary",), 40),
        name="matmul_post",
    )(a, w, x2, ng, mod5)


def _ffn_kernel(x_ref, g2_ref, sc_ref, sh_ref, wg_ref, wu_ref, wd_ref, g3_ref, gate_ref,
                o_ref, h_scr, acc_scr):
    f = pl.program_id(1)

    @pl.when(f == 0)
    def _():
        h_scr[...] = _norm_mod(x_ref[...], g2_ref[...], sc_ref[...], sh_ref[...]).astype(BF16)
        acc_scr[...] = jnp.zeros_like(acc_scr)

    h = h_scr[...]
    gt = _bdot(h, wg_ref[...].astype(BF16))
    up = _bdot(h, wu_ref[...].astype(BF16))
    act = (_silu(gt) * up).astype(BF16)
    acc_scr[...] += _bdot(act, wd_ref[...].astype(BF16))

    @pl.when(f == pl.num_programs(1) - 1)
    def _():
        o_ref[...] = x_ref[...] + gate_ref[...] * (_rms(acc_scr[...]) * g3_ref[...])


def _ffn(x2, ng, mod5, layer, w_gu, w_dn, *, seq, tm, tf):
    t, d = x2.shape
    ff = w_dn.shape[0]
    nf = ff // tf
    assert seq % tm == 0, "row tiles must not straddle batch rows"
    per_b = seq // tm
    mod_spec = lambda k: pl.BlockSpec((None, None, None, 1, d),
                                      lambda i, f: (layer, k, i // per_b, 0, 0))
    return pl.pallas_call(
        _ffn_kernel,
        grid=(t // tm, nf),
        in_specs=[pl.BlockSpec((tm, d), lambda i, f: (i, 0)),
                  pl.BlockSpec((None, 1, d), lambda i, f: (layer * 4 + 2, 0, 0)),
                  mod_spec(4), mod_spec(3),
                  pl.BlockSpec((d, tf), lambda i, f: (0, f)),
                  pl.BlockSpec((d, tf), lambda i, f: (0, nf + f)),
                  pl.BlockSpec((tf, d), lambda i, f: (f, 0)),
                  pl.BlockSpec((None, 1, d), lambda i, f: (layer * 4 + 3, 0, 0)),
                  mod_spec(5)],
        out_specs=pl.BlockSpec((tm, d), lambda i, f: (i, 0)),
        out_shape=jax.ShapeDtypeStruct((t, d), F32),
        scratch_shapes=[pltpu.VMEM((tm, d), BF16), pltpu.VMEM((tm, d), F32)],
        compiler_params=_cparams(("arbitrary", "arbitrary"), 56),
        name="ffn_swiglu",
    )(x2, ng, mod5, mod5, w_gu, w_gu, w_dn, ng, mod5)


def _router_kernel(x_ref, g2_ref, sc_ref, sh_ref, wr_ref, br_ref,
                   h_ref, gsel_ref, ridx_ref, cnt_ref, run_scr, *, n_exp):
    @pl.when(pl.program_id(0) == 0)
    def _():
        run_scr[...] = jnp.zeros_like(run_scr)

    h = _norm_mod(x_ref[...], g2_ref[...], sc_ref[...], sh_ref[...])
    h_ref[...] = h
    logits = jnp.dot(h, wr_ref[...], preferred_element_type=F32,
                     precision=lax.Precision.HIGHEST) + br_ref[...]
    tm = logits.shape[0]
    lane = lax.broadcasted_iota(jnp.int32, logits.shape, 1)
    logits = jnp.where(lane < n_exp, logits, -jnp.inf)
    ex = jnp.exp(logits - jnp.max(logits, axis=-1, keepdims=True))
    probs = ex / jnp.sum(ex, axis=-1, keepdims=True)
    p1 = jnp.max(probs, axis=-1, keepdims=True)
    i1 = jnp.min(jnp.where(probs == p1, lane, LANES), axis=-1, keepdims=True)
    rest = jnp.where(lane == i1, -1.0, probs)
    p2 = jnp.max(rest, axis=-1, keepdims=True)
    i2 = jnp.min(jnp.where(rest == p2, lane, LANES), axis=-1, keepdims=True)
    den = p1 + p2
    gsel_ref[...] = jnp.where(lane == 0, p1 / den, jnp.where(lane == 1, p2 / den, 0.0))

    sel = jnp.where((lane == i1) | (lane == i2), 1.0, 0.0)
    row = lax.broadcasted_iota(jnp.int32, (tm, tm), 0)
    col = lax.broadcasted_iota(jnp.int32, (tm, tm), 1)
    earlier = jnp.where(col < row, 1.0, 0.0).astype(BF16)
    rank = _bdot(earlier, sel.astype(BF16)) + run_scr[0:1, :]
    r1 = jnp.sum(jnp.where(lane == i1, rank, 0.0), axis=-1, keepdims=True).astype(jnp.int32)
    r2 = jnp.sum(jnp.where(lane == i2, rank, 0.0), axis=-1, keepdims=True).astype(jnp.int32)
    ridx_ref[...] = jnp.where(lane == 0, i1, jnp.where(lane == 1, i2,
                              jnp.where(lane == 2, r1, jnp.where(lane == 3, r2, 0))))
    run_scr[...] = run_scr[...] + jnp.sum(sel, axis=0, keepdims=True)
    cnt_ref[...] = run_scr[...]


def _router(x2, ng, mod5, layer, w_router, b_router, *, seq, tm):
    t, d = x2.shape
    n_exp = w_router.shape[1]
    assert seq % tm == 0, "row tiles must not straddle batch rows"
    per_b = seq // tm
    wr = jnp.zeros((d, LANES), F32).at[:, :n_exp].set(w_router.astype(F32))
    br = jnp.zeros((1, LANES), F32).at[0, :n_exp].set(b_router.astype(F32))
    mod_spec = lambda k: pl.BlockSpec((None, None, None, 1, d),
                                      lambda i: (layer, k, i // per_b, 0, 0))
    row_blk = lambda w: pl.BlockSpec((tm, w), lambda i: (i, 0))
    return pl.pallas_call(
        functools.partial(_router_kernel, n_exp=n_exp),
        grid=(t // tm,),
        in_specs=[row_blk(d),
                  pl.BlockSpec((None, 1, d), lambda i: (layer * 4 + 2, 0, 0)),
                  mod_spec(4), mod_spec(3),
                  pl.BlockSpec((d, LANES), lambda i: (0, 0)),
                  pl.BlockSpec((1, LANES), lambda i: (0, 0))],
        out_specs=[row_blk(d), row_blk(LANES), row_blk(LANES),
                   pl.BlockSpec((8, LANES), lambda i: (0, 0))],
        out_shape=[jax.ShapeDtypeStruct((t, d), F32),
                   jax.ShapeDtypeStruct((t, LANES), F32),
                   jax.ShapeDtypeStruct((t, LANES), jnp.int32),
                   jax.ShapeDtypeStruct((8, LANES), F32)],
        scratch_shapes=[pltpu.VMEM((8, LANES), F32)],
        compiler_params=_cparams(("arbitrary",), 32),
        name="moe_router",
    )(x2, ng, mod5, mod5, wr, br)


def _row_copy(src_ref, src_row, dst_ref, dst_row, sem):
    return pltpu.make_async_copy(src_ref.at[pl.ds(src_row, 1)], dst_ref.at[pl.ds(dst_row, 1)], sem)


def _dispatch_kernel(pa_ref, pb_ref, h_ref, hs_in_ref, hs_ref, sem):
    del hs_in_ref
    n = h_ref.shape[0]

    def issue(r, _):
        _row_copy(h_ref, r, hs_ref, pa_ref[r], sem).start()
        _row_copy(h_ref, r, hs_ref, pb_ref[r], sem).start()
        return 0

    def drain(r, _):
        _row_copy(h_ref, r, hs_ref, pa_ref[r], sem).wait()
        _row_copy(h_ref, r, hs_ref, pb_ref[r], sem).wait()
        return 0

    lax.fori_loop(0, n, issue, 0, unroll=8)
    lax.fori_loop(0, n, drain, 0, unroll=8)


def _dispatch(h, pos_a, pos_b, hs_zero, *, tm):
    t, d = h.shape
    smem_blk = pl.BlockSpec((tm,), lambda i: (i,), memory_space=pltpu.SMEM)
    return pl.pallas_call(
        _dispatch_kernel,
        grid=(t // tm,),
        in_specs=[smem_blk, smem_blk,
                  pl.BlockSpec((tm, d), lambda i: (i, 0)),
                  pl.BlockSpec(memory_space=pl.ANY)],
        out_specs=pl.BlockSpec(memory_space=pl.ANY),
        out_shape=jax.ShapeDtypeStruct(hs_zero.shape, hs_zero.dtype),
        scratch_shapes=[pltpu.SemaphoreType.DMA],
        input_output_aliases={3: 0},
        compiler_params=_cparams(("arbitrary",), 24),
        name="moe_dispatch",
    )(pos_a, pos_b, h, hs_zero)


def _expert_kernel(te_ref, nu_ref, hs_ref, wg_ref, wu_ref, wd_ref, y_ref, h_scr, acc_scr):
    del te_ref
    f = pl.program_id(1)
    last = pl.num_programs(1) - 1
    used = pl.program_id(0) < nu_ref[0]

    @pl.when(jnp.logical_not(used) & (f == last))
    def _():
        y_ref[...] = jnp.zeros_like(y_ref)

    @pl.when(used)
    def _():
        @pl.when(f == 0)
        def _():
            h_scr[...] = hs_ref[...].astype(BF16)
            acc_scr[...] = jnp.zeros_like(acc_scr)

        h = h_scr[...]
        gt = _bdot(h, wg_ref[...].astype(BF16))
        up = _bdot(h, wu_ref[...].astype(BF16))
        act = (_silu(gt) * up).astype(BF16)
        acc_scr[...] += _bdot(act, wd_ref[...].astype(BF16))

        @pl.when(f == last)
        def _():
            y_ref[...] = acc_scr[...]


def _experts(hs, tile_expert, n_used, w_gu, w_dn, *, tm, tf):
    npad, d = hs.shape
    n_exp, ff, _ = w_dn.shape
    nf = ff // tf
    row = lambda i, f, te, nu: (jnp.minimum(i, nu[0] - 1), 0)
    fe = lambda i, f, nu: jnp.where(i < nu[0], f, nf - 1)
    return pl.pallas_call(
        _expert_kernel,
        grid_spec=pltpu.PrefetchScalarGridSpec(
            num_scalar_prefetch=2,
            grid=(npad // tm, nf),
            in_specs=[pl.BlockSpec((tm, d), row),
                      pl.BlockSpec((None, d, tf), lambda i, f, te, nu: (te[i], 0, fe(i, f, nu))),
                      pl.BlockSpec((None, d, tf),
                                   lambda i, f, te, nu: (te[i], 0, nf + fe(i, f, nu))),
                      pl.BlockSpec((None, tf, d), lambda i, f, te, nu: (te[i], fe(i, f, nu), 0))],
            out_specs=pl.BlockSpec((tm, d), lambda i, f, te, nu: (i, 0)),
            scratch_shapes=[pltpu.VMEM((tm, d), BF16), pltpu.VMEM((tm, d), F32)]),
        out_shape=jax.ShapeDtypeStruct((npad, d), F32),
        compiler_params=_cparams(("arbitrary", "arbitrary"), 56),
        name="moe_experts",
    )(tile_expert, n_used, hs, w_gu, w_gu, w_dn)


def _combine_kernel(pa_ref, pb_ref, x_ref, gsel_ref, g3_ref, gate_ref, ys_ref, o_ref,
                    ya_scr, yb_scr, sem):
    n = x_ref.shape[0]

    def issue(r, _):
        _row_copy(ys_ref, pa_ref[r], ya_scr, r, sem).start()
        _row_copy(ys_ref, pb_ref[r], yb_scr, r, sem).start()
        return 0

    def drain(r, _):
        _row_copy(ys_ref, pa_ref[r], ya_scr, r, sem).wait()
        _row_copy(ys_ref, pb_ref[r], yb_scr, r, sem).wait()
        return 0

    lax.fori_loop(0, n, issue, 0, unroll=8)
    lax.fori_loop(0, n, drain, 0, unroll=8)
    gs = gsel_ref[...]
    y = gs[:, 0:1] * ya_scr[...] + gs[:, 1:2] * yb_scr[...]
    o_ref[...] = x_ref[...] + gate_ref[...] * (_rms(y) * g3_ref[...])


def _combine(x2, gsel, pos_a, pos_b, ys, ng, mod5, layer, *, seq, tm):
    t, d = x2.shape
    assert seq % tm == 0, "row tiles must not straddle batch rows"
    per_b = seq // tm
    smem_blk = pl.BlockSpec((tm,), lambda i: (i,), memory_space=pltpu.SMEM)
    return pl.pallas_call(
        _combine_kernel,
        grid=(t // tm,),
        in_specs=[smem_blk, smem_blk,
                  pl.BlockSpec((tm, d), lambda i: (i, 0)),
                  pl.BlockSpec((tm, LANES), lambda i: (i, 0)),
                  pl.BlockSpec((None, 1, d), lambda i: (layer * 4 + 3, 0, 0)),
                  pl.BlockSpec((None, None, None, 1, d),
                               lambda i: (layer, 5, i // per_b, 0, 0)),
                  pl.BlockSpec(memory_space=pl.ANY)],
        out_specs=pl.BlockSpec((tm, d), lambda i: (i, 0)),
        out_shape=jax.ShapeDtypeStruct((t, d), F32),
        scratch_shapes=[pltpu.VMEM((tm, d), F32), pltpu.VMEM((tm, d), F32),
                        pltpu.SemaphoreType.DMA],
        compiler_params=_cparams(("arbitrary",), 40),
        name="moe_combine",
    )(pos_a, pos_b, x2, gsel, ng, mod5, ys)


def _moe(x2, ng, mod5, layer, w_router, b_router, w_gu, w_dn, *, seq, tm_tok, tm_exp, tf):
    t, d = x2.shape
    n_exp = w_router.shape[1]
    h, gsel, ridx, cnt = _router(x2, ng, mod5, layer, w_router, b_router, seq=seq, tm=tm_tok)

    counts = cnt[0, :n_exp].astype(jnp.int32)
    padded = ((counts + tm_exp - 1) // tm_exp) * tm_exp
    ends = jnp.cumsum(padded)
    starts = ends - padded
    pos_a = starts[ridx[:, 0]] + ridx[:, 2]
    pos_b = starts[ridx[:, 1]] + ridx[:, 3]
    n_tiles = (2 * t + n_exp * (tm_exp - 1)) // tm_exp
    n_used = (ends[-1] // tm_exp).reshape(1)
    tile_row = jnp.arange(n_tiles, dtype=jnp.int32) * tm_exp
    tile_expert = jnp.sum(tile_row[:, None] >= ends[None, :], axis=1).astype(jnp.int32)
    tile_expert = jnp.where(jnp.arange(n_tiles) < n_used[0], tile_expert,
                            tile_expert[n_used[0] - 1])

    hs = _dispatch(h, pos_a, pos_b, jnp.zeros((n_tiles * tm_exp, d), F32), tm=tm_tok)
    ys = _experts(hs, tile_expert, n_used, w_gu, w_dn, tm=tm_exp, tf=tf)
    return _combine(x2, gsel, pos_a, pos_b, ys, ng, mod5, layer, seq=seq, tm=tm_tok)


def _attn_kernel(q_ref, k_ref, v_ref, o_ref, acc_scr, *, tq, hd, pairs, scale):
    seq = q_ref.shape[0]
    row = lax.broadcasted_iota(jnp.int32, (tq, tq), 0)
    col = lax.broadcasted_iota(jnp.int32, (tq, tq), 1)
    strict = col < row
    after_mat = jnp.where(row > col, 1.0, 0.0).astype(BF16)
    lane = lax.broadcasted_iota(jnp.int32, (tq, LANES), 1)
    first_head = lane < hd
    heads = [(p, h) for p in range(pairs) for h in range(2)]

    def kv_block(qhs, k0, drops, masked):
        new_drops = []
        for n, (p, _) in enumerate(heads):
            lanes = slice(p * LANES, (p + 1) * LANES)
            z = lax.dot_general(qhs[n], k_ref[pl.ds(k0, tq), lanes], (((1,), (1,)), ((), ())),
                                preferred_element_type=F32)
            sp = jnp.maximum(z, 0.0) + jnp.log(1.0 + jnp.exp(-jnp.abs(z)))
            if masked:
                sp = jnp.where(strict, sp, 0.0)
            drop_after = _bdot(sp.astype(BF16), after_mat) + drops[n]
            w = jnp.exp((z - sp) - drop_after)
            if masked:
                w = jnp.where(strict, w, 0.0)
            acc_scr[n] += _bdot(w.astype(BF16), v_ref[pl.ds(k0, tq), lanes])
            new_drops.append(drops[n] + jnp.sum(sp, axis=-1, keepdims=True))
        return tuple(new_drops)

    def q_block(i, _):
        q0 = pl.multiple_of(i * tq, tq)
        qhs = []
        for p, h in heads:
            q2 = q_ref[pl.ds(q0, tq), p * LANES:(p + 1) * LANES] * scale
            qhs.append(jnp.where(first_head == (h == 0), q2, jnp.zeros_like(q2)))
        acc_scr[...] = jnp.zeros_like(acc_scr)
        zero = jnp.zeros((tq, 1), F32)
        drops = kv_block(qhs, q0, (zero,) * len(heads), True)

        def live(drops):
            m = functools.reduce(jnp.minimum, drops)
            return (jnp.min(m) < DROP_DEAD).astype(jnp.int32)

        def cond(state):
            return (state[0] < i) & (state[1] > 0)

        def body(state):
            k0 = pl.multiple_of((i - 1 - state[0]) * tq, tq)
            drops = kv_block(qhs, k0, state[2:], False)
            return (state[0] + 1, live(drops)) + drops

        lax.while_loop(cond, body, (jnp.int32(0), live(drops)) + drops)
        for p in range(pairs):
            o_ref[pl.ds(q0, tq), p * LANES:(p + 1) * LANES] = jnp.where(
                first_head, acc_scr[2 * p], acc_scr[2 * p + 1]).astype(o_ref.dtype)
        return 0

    lax.fori_loop(0, seq // tq, q_block, 0)


def _attention(qkv, *, hd, tq, pairs):
    bsz, seq, three_d = qkv.shape
    d = three_d // 3
    wl = pairs * LANES
    n_step = d // wl
    blk = lambda off: pl.BlockSpec((None, seq, wl), lambda b, p: (b, 0, off + p))
    return pl.pallas_call(
        functools.partial(_attn_kernel, tq=tq, hd=hd, pairs=pairs, scale=1.0 / math.sqrt(hd)),
        grid=(bsz, n_step),
        in_specs=[blk(0), blk(n_step), blk(2 * n_step)],
        out_specs=pl.BlockSpec((None, seq, wl), lambda b, p: (b, 0, p)),
        out_shape=jax.ShapeDtypeStruct((bsz, seq, d), BF16),
        scratch_shapes=[pltpu.VMEM((2 * pairs, tq, LANES), F32)],
        compiler_params=_cparams(("arbitrary", "arbitrary"), 32),
        name="stickbreaking_attention",
    )(qkv, qkv, qkv)


def kernel(x, c, ada_w, ada_b, norm_g, s5_w_in, s5_lambda_re, s5_lambda_im, s5_b_re, s5_b_im,
           s5_c_re, s5_c_im, s5_d, s5_log_step, s5_w_out, sb_w_qkv, sb_w_out, ffn_w_gate_up,
           ffn_w_down, moe_w_router, moe_b_router, moe_w_gate_up, moe_w_down):
    bsz, seq, d = x.shape
    depth = ada_w.shape[0]
    t = bsz * seq
    hd = 64
    tm = min(1024, seq)
    tf = 512

    mod = _adaln(c, ada_w, ada_b)
    mod5 = mod.reshape(depth, 6, bsz, 1, d)
    ng = norm_g.reshape(depth * 4, 1, d)

    bbd, cbd, lam, dsk = _s5_discretise(s5_lambda_re[0], s5_lambda_im[0], s5_b_re[0], s5_b_im[0],
                                        s5_c_re[0], s5_c_im[0], s5_d[0], s5_log_step[0])
    x1 = _s5_layer(x, ng, mod, 0, s5_w_in[0].astype(BF16), bbd, cbd, lam, dsk,
                   s5_w_out[0].astype(BF16), tc=64)
    x2 = _ffn(x1.reshape(t, d), ng, mod5, 0, ffn_w_gate_up[0], ffn_w_down[0],
              seq=seq, tm=tm, tf=tf)

    qkv = _nm_mm(x2, ng, mod5, 1, sb_w_qkv[0].astype(BF16), seq=seq, tm=tm, out_dtype=BF16)
    o = _attention(qkv.reshape(bsz, seq, 3 * d), hd=hd, tq=256, pairs=4)
    x3 = _mm_post(o.reshape(t, d), sb_w_out[0].astype(BF16), x2, ng, mod5, 1, seq=seq, tm=tm)
    x4 = _moe(x3, ng, mod5, 1, moe_w_router[0], moe_b_router[0], moe_w_gate_up[0],
              moe_w_down[0], seq=seq, tm_tok=tm, tm_exp=1024, tf=tf)
    return x4.reshape(bsz, seq, d)
```
